```python
import math
import jax, jax.numpy as jnp
from jax import lax
import numpy as np

D_MODEL = 1024
BATCH = 8
SEQ = 4096
DEPTH = 2

EPS = 1e-6
D_FF = 2752
PLE_DIM = 256

S5_WIDTH = 512
S5_GROUP = 16
S5_GROUPS = S5_WIDTH // S5_GROUP
S5_STATE = 64

GLA_HEADS = 4
GLA_DK = 64
GLA_DV = 128
GLA_KEY = GLA_HEADS * GLA_DK
GLA_VAL = GLA_HEADS * GLA_DV
GLA_RANK = 16
GLA_GATE_NORM = 16.0
GLA_CHUNK = 64

SSD_HEADS = 8
SSD_HEADDIM = 64
SSD_INNER = SSD_HEADS * SSD_HEADDIM
SSD_GROUPS = 2
SSD_STATE = 128
SSD_CONV = 4
SSD_CHUNK = 128
SSD_CONV_DIM = SSD_INNER + 2 * SSD_GROUPS * SSD_STATE

N_BRANCH = 3
IN_SIZES = (S5_WIDTH, GLA_KEY, GLA_KEY, GLA_VAL, GLA_VAL, GLA_RANK,
            SSD_INNER, SSD_CONV_DIM, SSD_HEADS, D_MODEL, D_MODEL, D_MODEL)
IN_TOTAL = sum(IN_SIZES)

kernel_name = "hybrid_s5_gla_ssd_gated_macaron"


def rmsnorm(x, w):
    xf = x.astype(jnp.float32)
    r = lax.rsqrt(jnp.mean(xf * xf, axis=-1, keepdims=True) + EPS)
    return (xf * r * w.astype(jnp.float32)).astype(x.dtype)


def swiglu(x, w_gate, w_up, w_down):
    return (jax.nn.silu(x @ w_gate) * (x @ w_up)) @ w_down


def _cplx_combine(e1, e2):
    ar1, ai1, br1, bi1 = e1
    ar2, ai2, br2, bi2 = e2
    return (ar1 * ar2 - ai1 * ai2,
            ar1 * ai2 + ai1 * ar2,
            ar2 * br1 - ai2 * bi1 + br2,
            ar2 * bi1 + ai2 * br1 + bi2)


def s5_mixer(u, lam_re, lam_im, log_step, b_re, b_im, c_re, c_im, d_skip, w_glu):
    bsz, s, _ = u.shape
    uf = u.astype(jnp.float32)
    ug = uf.reshape(bsz, s, S5_GROUPS, S5_GROUP)
    lr = jnp.minimum(lam_re.astype(jnp.float32), -1e-4)
    li = lam_im.astype(jnp.float32)
    step = jnp.exp(log_step.astype(jnp.float32))[:, None]
    mag = jnp.exp(lr * step)
    ar = mag * jnp.cos(li * step)
    ai = mag * jnp.sin(li * step)
    den = lr * lr + li * li
    nr = ar - 1.0
    fr = (nr * lr + ai * li) / den
    fi = (ai * lr - nr * li) / den
    br = b_re.astype(jnp.float32)
    bi = b_im.astype(jnp.float32)
    bbr = fr[..., None] * br - fi[..., None] * bi
    bbi = fr[..., None] * bi + fi[..., None] * br
    xr = jnp.einsum('bsgh,gph->sbgp', ug, bbr)
    xi = jnp.einsum('bsgh,gph->sbgp', ug, bbi)
    a_r = jnp.broadcast_to(ar[None, None], (s, 1) + ar.shape)
    a_i = jnp.broadcast_to(ai[None, None], (s, 1) + ai.shape)
    _, _, hr, hi = lax.associative_scan(_cplx_combine, (a_r, a_i, xr, xi), axis=0)
    y = (jnp.einsum('sbgp,ghp->bsgh', hr, c_re.astype(jnp.float32))
         - jnp.einsum('sbgp,ghp->bsgh', hi, c_im.astype(jnp.float32)))
    y = y.reshape(bsz, s, S5_WIDTH) + d_skip.astype(jnp.float32) * uf
    g = jax.nn.gelu(y).astype(u.dtype)
    return g * jax.nn.sigmoid(g @ w_glu)


def _gla_chunk(state, inp):
    qc, kc, vc, bc = inp
    n = qc.shape[2]
    causal = jnp.tril(jnp.ones((n, n), dtype=bool))
    inter = jnp.einsum('bhik,bhkv->bhiv', qc * jnp.exp(bc), state)
    diff = bc[:, :, :, None, :] - bc[:, :, None, :, :]
    decay = jnp.exp(jnp.where(causal[:, :, None], diff, -jnp.inf))
    attn = jnp.einsum('bhijk,bhjk->bhij', qc[:, :, :, None, :] * decay, kc)
    intra = jnp.einsum('bhij,bhjv->bhiv', attn, vc)
    b_last = bc[:, :, -1:, :]
    new_state = (jnp.exp(b_last[:, :, 0, :])[..., None] * state
                 + jnp.einsum('bhjk,bhjv->bhkv', kc * jnp.exp(b_last - bc), vc))
    return new_state, inter + intra


def gla_mixer(q, k, v, g_out, gate_lr, w_gate2, b_gate2, norm_w):
    bsz, s, _ = q.shape
    nc = s // GLA_CHUNK
    log_a = jax.nn.log_sigmoid((gate_lr @ w_gate2 + b_gate2).astype(jnp.float32)) / GLA_GATE_NORM

    def to_chunks(t, d):
        t = t.astype(jnp.float32).reshape(bsz, nc, GLA_CHUNK, GLA_HEADS, d)
        return t.transpose(1, 0, 3, 2, 4)

    qc = to_chunks(q, GLA_DK) * (GLA_DK ** -0.5)
    kc = to_chunks(k, GLA_DK)
    vc = to_chunks(v, GLA_DV)
    bc = jnp.cumsum(to_chunks(log_a, GLA_DK), axis=3)
    state0 = jnp.zeros((bsz, GLA_HEADS, GLA_DK, GLA_DV), jnp.float32)
    _, o = lax.scan(_gla_chunk, state0, (qc, kc, vc, bc))
    o = o.transpose(1, 0, 3, 2, 4).reshape(bsz, s, GLA_HEADS, GLA_DV)
    o = rmsnorm(o, norm_w).reshape(bsz, s, GLA_VAL)
    return (o * jax.nn.silu(g_out.astype(jnp.float32))).astype(q.dtype)


def causal_dwconv(x, w, b):
    c = x.shape[-1]
    out = lax.conv_general_dilated(x, w[:, None, :], window_strides=(1,),
                                   padding=((SSD_CONV - 1, 0),),
                                   dimension_numbers=('NWC', 'WIO', 'NWC'),
                                   feature_group_count=c)
    return out + b


def _ssd_chunk_state(h, inp):
    da, st = inp
    return jnp.exp(da)[..., None, None] * h + st, h


def ssd_mixer(z, xbc, dt_raw, conv_w, conv_b, dt_bias, a_log, d_skip, norm_w):
    bsz, s, _ = z.shape
    nc = s // SSD_CHUNK
    rep = SSD_HEADS // SSD_GROUPS
    xbc = jax.nn.silu(causal_dwconv(xbc, conv_w, conv_b)).astype(jnp.float32)
    xs = xbc[..., :SSD_INNER].reshape(bsz, s, SSD_HEADS, SSD_HEADDIM)
    bm = xbc[..., SSD_INNER:SSD_INNER + SSD_GROUPS * SSD_STATE].reshape(bsz, nc, SSD_CHUNK, SSD_GROUPS, SSD_STATE)
    cm = xbc[..., SSD_INNER + SSD_GROUPS * SSD_STATE:].reshape(bsz, nc, SSD_CHUNK, SSD_GROUPS, SSD_STATE)
    dt = jax.nn.softplus((dt_raw + dt_bias).astype(jnp.float32))
    a = -jnp.exp(a_log.astype(jnp.float32))
    la = (dt * a).reshape(bsz, nc, SSD_CHUNK, SSD_GROUPS, rep)
    xdt = (xs * dt[..., None]).reshape(bsz, nc, SSD_CHUNK, SSD_GROUPS, rep, SSD_HEADDIM)
    cum = jnp.cumsum(la, axis=2)
    causal = jnp.tril(jnp.ones((SSD_CHUNK, SSD_CHUNK), dtype=bool))
    diff = cum[:, :, :, None] - cum[:, :, None, :]
    decay = jnp.exp(jnp.where(causal[:, :, None, None], diff, -jnp.inf))
    scores = jnp.einsum('bclgn,bcmgn->bclmg', cm, bm)
    y_diag = jnp.einsum('bclmgr,bcmgrp->bclgrp', scores[..., None] * decay, xdt)
    decay_states = jnp.exp(cum[:, :, -1:] - cum)
    states = jnp.einsum('bclgn,bclgr,bclgrp->bcgrpn', bm, decay_states, xdt)
    h0 = jnp.zeros((bsz, SSD_GROUPS, rep, SSD_HEADDIM, SSD_STATE), jnp.float32)
    _, h_in = lax.scan(_ssd_chunk_state, h0,
                       (cum[:, :, -1].transpose(1, 0, 2, 3), states.transpose(1, 0, 2, 3, 4, 5)))
    h_in = h_in.transpose(1, 0, 2, 3, 4, 5)
    y_off = jnp.einsum('bclgn,bcgrpn,bclgr->bclgrp', cm, h_in, jnp.exp(cum))
    y = (y_diag + y_off).reshape(bsz, s, SSD_HEADS, SSD_HEADDIM)
    y = y + d_skip.astype(jnp.float32)[:, None] * xs
    y = y.reshape(bsz, s, SSD_INNER) * jax.nn.silu(z.astype(jnp.float32))
    y = rmsnorm(y.reshape(bsz, s, SSD_GROUPS, SSD_INNER // SSD_GROUPS),
                norm_w.reshape(SSD_GROUPS, SSD_INNER // SSD_GROUPS))
    return y.reshape(bsz, s, SSD_INNER).astype(z.dtype)


def hybrid_mix(u, w_in, s5_lam_re, s5_lam_im, s5_log_step, s5_b_re, s5_b_im, s5_c_re, s5_c_im,
               s5_d, s5_glu, gla_gate_w2, gla_gate_b2, gla_norm, ssd_conv_w, ssd_conv_b,
               ssd_dt_bias, ssd_a_log, ssd_d, ssd_norm, w_br_s5, w_br_gla, w_br_ssd, w_out):
    split_pts = np.cumsum(IN_SIZES)[:-1].tolist()
    (u_s5, q, k, v, g_out, gate_lr, z, xbc, dt_raw,
     gate_s5, gate_gla, gate_ssd) = jnp.split(u @ w_in, split_pts, axis=-1)
    y_s5 = s5_mixer(u_s5, s5_lam_re, s5_lam_im, s5_log_step, s5_b_re, s5_b_im,
                    s5_c_re, s5_c_im, s5_d, s5_glu)
    y_gla = gla_mixer(q, k, v, g_out, gate_lr, gla_gate_w2, gla_gate_b2, gla_norm)
    y_ssd = ssd_mixer(z, xbc, dt_raw, ssd_conv_w, ssd_conv_b, ssd_dt_bias, ssd_a_log, ssd_d, ssd_norm)
    merged = (jax.nn.sigmoid(gate_s5) * (y_s5 @ w_br_s5)
              + jax.nn.sigmoid(gate_gla) * (y_gla @ w_br_gla)
              + jax.nn.sigmoid(gate_ssd) * (y_ssd @ w_br_ssd))
    return merged @ w_out


def setup_inputs(seed: int = 0) -> dict:
    key = jax.random.key(seed)
    ks = iter(jax.random.split(key, 48))
    L = DEPTH

    def nrm(shape, scale):
        return jax.random.normal(next(ks), shape, jnp.float32) * scale

    def gain(shape):
        return 1.0 + nrm(shape, 0.01)

    def unif(shape, lo, hi):
        return jax.random.uniform(next(ks), shape, jnp.float32, lo, hi)

    n_idx = jnp.arange(S5_STATE, dtype=jnp.float32)
    dt_init = jnp.exp(unif((L, SSD_HEADS), math.log(1e-3), math.log(1e-1)))
    return {
        "x": nrm((BATCH, SEQ, D_MODEL), 1.0),
        "p": nrm((DEPTH, BATCH, SEQ, PLE_DIM), 1.0),
        "ffn1_norm": gain((L, D_MODEL)),
        "ffn1_gate": nrm((L, D_MODEL, D_FF), D_MODEL ** -0.5),
        "ffn1_up": nrm((L, D_MODEL, D_FF), D_MODEL ** -0.5),
        "ffn1_down": nrm((L, D_FF, D_MODEL), D_FF ** -0.5),
        "mix_norm": gain((L, D_MODEL)),
        "w_in": nrm((L, D_MODEL, IN_TOTAL), D_MODEL ** -0.5),
        "s5_lam_re": -0.5 + nrm((L, S5_GROUPS, S5_STATE), 0.01),
        "s5_lam_im": math.pi * n_idx + nrm((L, S5_GROUPS, S5_STATE), 0.01),
        "s5_log_step": unif((L, S5_GROUPS), math.log(1e-3), math.log(1e-1)),
        "s5_b_re": nrm((L, S5_GROUPS, S5_STATE, S5_GROUP), (2.0 * S5_GROUP) ** -0.5),
        "s5_b_im": nrm((L, S5_GROUPS, S5_STATE, S5_GROUP), (2.0 * S5_GROUP) ** -0.5),
        "s5_c_re": nrm((L, S5_GROUPS, S5_GROUP, S5_STATE), S5_STATE ** -0.5),
        "s5_c_im": nrm((L, S5_GROUPS, S5_GROUP, S5_STATE), S5_STATE ** -0.5),
        "s5_d": nrm((L, S5_WIDTH), 1.0),
        "s5_glu": nrm((L, S5_WIDTH, S5_WIDTH), S5_WIDTH ** -0.5),
        "gla_gate_w2": nrm((L, GLA_RANK, GLA_KEY), GLA_RANK ** -0.5),
        "gla_gate_b2": nrm((L, GLA_KEY), 0.1),
        "gla_norm": gain((L, GLA_DV)),
        "ssd_conv_w": nrm((L, SSD_CONV, SSD_CONV_DIM), SSD_CONV ** -0.5),
        "ssd_conv_b": nrm((L, SSD_CONV_DIM), 0.01),
        "ssd_dt_bias": dt_init + jnp.log(-jnp.expm1(-dt_init)),
        "ssd_a_log": jnp.log(unif((L, SSD_HEADS), 1.0, 16.0)),
        "ssd_d": gain((L, SSD_HEADS)),
        "ssd_norm": gain((L, SSD_INNER)),
        "w_br_s5": nrm((L, S5_WIDTH, D_MODEL), S5_WIDTH ** -0.5),
        "w_br_gla": nrm((L, GLA_VAL, D_MODEL), GLA_VAL ** -0.5),
        "w_br_ssd": nrm((L, SSD_INNER, D_MODEL), SSD_INNER ** -0.5),
        "w_out": nrm((L, D_MODEL, D_MODEL), D_MODEL ** -0.5),
        "ffn2_norm": gain((L, D_MODEL)),
        "ffn2_gate": nrm((L, D_MODEL, D_FF), D_MODEL ** -0.5),
        "ffn2_up": nrm((L, D_MODEL, D_FF), D_MODEL ** -0.5),
        "ffn2_down": nrm((L, D_FF, D_MODEL), D_FF ** -0.5),
        "ple_norm": gain((L, D_MODEL)),
        "ple_gate": nrm((L, D_MODEL, D_MODEL), D_MODEL ** -0.5),
        "ple_proj": nrm((L, PLE_DIM, D_MODEL), PLE_DIM ** -0.5),
        "final_norm": gain((D_MODEL,)),
    }


def reference(x, p, ffn1_norm, ffn1_gate, ffn1_up, ffn1_down, mix_norm, w_in,
              s5_lam_re, s5_lam_im, s5_log_step, s5_b_re, s5_b_im, s5_c_re, s5_c_im, s5_d, s5_glu,
              gla_gate_w2, gla_gate_b2, gla_norm, ssd_conv_w, ssd_conv_b, ssd_dt_bias, ssd_a_log,
              ssd_d, ssd_norm, w_br_s5, w_br_gla, w_br_ssd, w_out, ffn2_norm, ffn2_gate, ffn2_up,
              ffn2_down, ple_norm, ple_gate, ple_proj, final_norm):
    h = x
    for i in range(DEPTH):
        h = h + 0.5 * swiglu(rmsnorm(h, ffn1_norm[i]), ffn1_gate[i], ffn1_up[i], ffn1_down[i])
        h = h + hybrid_mix(rmsnorm(h, mix_norm[i]), w_in[i],
                           s5_lam_re[i], s5_lam_im[i], s5_log_step[i], s5_b_re[i], s5_b_im[i],
                           s5_c_re[i], s5_c_im[i], s5_d[i], s5_glu[i],
                           gla_gate_w2[i], gla_gate_b2[i], gla_norm[i],
                           ssd_conv_w[i], ssd_conv_b[i], ssd_dt_bias[i], ssd_a_log[i], ssd_d[i], ssd_norm[i],
                           w_br_s5[i], w_br_gla[i], w_br_ssd[i], w_out[i])
        h = h + 0.5 * swiglu(rmsnorm(h, ffn2_norm[i]), ffn2_gate[i], ffn2_up[i], ffn2_down[i])
        h = h + jax.nn.sigmoid(rmsnorm(h, ple_norm[i]) @ ple_gate[i]) * (p[i] @ ple_proj[i])
    return rmsnorm(h, final_norm)
```

```python
import functools

import numpy as np
import jax
import jax.numpy as jnp
from jax import lax
from jax.experimental import pallas as pl
from jax.experimental.pallas import tpu as pltpu

F32 = jnp.float32
BF16 = jnp.bfloat16

EPS = 1e-6
D_MODEL = 1024
D_FF = 2752
PLE_DIM = 256

S5_WIDTH = 512
S5_GROUP = 16
S5_GROUPS = 32
S5_STATE = 64
S5_NSTATE = S5_GROUPS * S5_STATE

GLA_HEADS = 4
GLA_DK = 64
GLA_DV = 128
GLA_KEY = 256
GLA_VAL = 512
GLA_RANK = 16
GLA_GATE_NORM = 16.0

SSD_HEADS = 8
SSD_HEADDIM = 64
SSD_INNER = 512
SSD_GROUPS = 2
SSD_STATE = 128
SSD_CONV = 4
SSD_CONV_DIM = 1024

_IN_SIZES = (S5_WIDTH, GLA_KEY, GLA_KEY, GLA_VAL, GLA_VAL, GLA_RANK,
             SSD_INNER, SSD_CONV_DIM, SSD_HEADS, D_MODEL, D_MODEL, D_MODEL)
_IN_OFFS = tuple(int(v) for v in np.concatenate([[0], np.cumsum(_IN_SIZES)]))

LANES = 128
SUBLANES = 8
VMEM_BYTES_V7X = 64 * 1024 * 1024

FFN_ROWS = 512
FFN_SPLIT = 2
D_FF_PAD = 2816
S5_ROWS = 512
GLA_ROWS = 256
GLA_CHUNK = 64
SSD_ROWS = 256
SSD_CHUNK = 128
MERGE_ROWS = 512
PLE_ROWS = 512


def _cparams(vmem_mib, ndims):
    return pltpu.CompilerParams(
        dimension_semantics=("arbitrary",) * ndims,
        vmem_limit_bytes=vmem_mib * 1024 * 1024)


def _const_spec(arr, ngrid):
    zeros = (0,) * arr.ndim
    if ngrid == 1:
        imap = lambda i: zeros
    else:
        imap = lambda b, c: zeros
    return pl.BlockSpec(arr.shape, imap, pipeline_mode=pl.Buffered(1))


def _rms(x, w):
    ms = jnp.mean(x * x, axis=-1, keepdims=True)
    return x * lax.rsqrt(ms + EPS) * w


def _silu(x):
    return x * jax.nn.sigmoid(x)


def _softplus(x):
    return jnp.maximum(x, 0.0) + jnp.log1p(jnp.exp(-jnp.abs(x)))


def _log_sigmoid(x):
    return jnp.minimum(x, 0.0) - jnp.log1p(jnp.exp(-jnp.abs(x)))


def _dot(a, b):
    return jnp.dot(a, b, preferred_element_type=F32)


def _dot_nt(a, b):
    return lax.dot_general(a, b, (((1,), (1,)), ((), ())), preferred_element_type=F32)


def _dot_tn(a, b):
    return lax.dot_general(a, b, (((0,), (0,)), ((), ())), preferred_element_type=F32)


def _split_bf16(x):
    hi = x.astype(BF16)
    lo = (x - hi.astype(F32)).astype(BF16)
    return hi, lo


def _ffn_kernel(h_ref, nw_ref, wg_ref, wu_ref, wd_ref, o_ref):
    x = h_ref[...]
    xn = _rms(x, nw_ref[...]).astype(BF16)
    fc = wg_ref.shape[1] // FFN_SPLIT
    acc = jnp.zeros_like(x)
    for c in range(FFN_SPLIT):
        g = _dot(xn, wg_ref[:, c * fc:(c + 1) * fc])
        u = _dot(xn, wu_ref[:, c * fc:(c + 1) * fc])
        hid = (_silu(g) * u).astype(BF16)
        acc = acc + _dot(hid, wd_ref[c * fc:(c + 1) * fc, :])
    o_ref[...] = x + 0.5 * acc


def _ffn(h, nw, wg, wu, wd):
    n, d = h.shape
    rows = min(FFN_ROWS, n)
    row_spec = pl.BlockSpec((rows, d), lambda i: (i, 0))
    return pl.pallas_call(
        _ffn_kernel,
        grid=(n // rows,),
        in_specs=[row_spec, _const_spec(nw, 1), _const_spec(wg, 1), _const_spec(wu, 1),
                  _const_spec(wd, 1)],
        out_specs=row_spec,
        out_shape=jax.ShapeDtypeStruct((n, d), F32),
        compiler_params=_cparams(52, 1),
        name="ffn",
    )(h, nw, wg, wu, wd)


def _s5_scan_level(r_ref, nblk, tab_ref, lev, e_ref):
    p = S5_NSTATE

    def body(k, carry):
        r0 = pl.multiple_of(k * SUBLANES, SUBLANES)
        xr = r_ref[pl.ds(r0, SUBLANES), 0:p]
        xi = r_ref[pl.ds(r0, SUBLANES), p:2 * p]
        for j, d in enumerate((1, 2, 4)):
            pr = tab_ref[lev, j, :, 0:p]
            pi = tab_ref[lev, j, :, p:2 * p]
            sr = pltpu.roll(xr, d, 0)
            si = pltpu.roll(xi, d, 0)
            xr, xi = xr + pr * sr - pi * si, xi + pr * si + pi * sr
        r_ref[pl.ds(r0, SUBLANES), 0:p] = xr
        r_ref[pl.ds(r0, SUBLANES), p:2 * p] = xi
        if e_ref is not None:
            e_ref[pl.ds(k, 1), 0:p] = xr[SUBLANES - 1:SUBLANES, :]
            e_ref[pl.ds(k, 1), p:2 * p] = xi[SUBLANES - 1:SUBLANES, :]
        return carry

    lax.fori_loop(0, nblk, body, 0)


def _s5_fixup(r_ref, nblk, tab_ref, lev, prev_ref, s_ref):
    p = S5_NSTATE

    def apply(k, pr, pi):
        r0 = pl.multiple_of(k * SUBLANES, SUBLANES)
        cr = tab_ref[lev, 3, :, 0:p]
        ci = tab_ref[lev, 3, :, p:2 * p]
        xr = r_ref[pl.ds(r0, SUBLANES), 0:p]
        xi = r_ref[pl.ds(r0, SUBLANES), p:2 * p]
        r_ref[pl.ds(r0, SUBLANES), 0:p] = xr + cr * pr - ci * pi
        r_ref[pl.ds(r0, SUBLANES), p:2 * p] = xi + cr * pi + ci * pr

    apply(0, s_ref[0:1, 0:p], s_ref[0:1, p:2 * p])

    def body(k, carry):
        apply(k, prev_ref[pl.ds(k - 1, 1), 0:p], prev_ref[pl.ds(k - 1, 1), p:2 * p])
        return carry

    if nblk > 1:
        lax.fori_loop(1, nblk, body, 0)


def _s5_kernel(h_ref, nw_ref, win_ref, bblk_ref, cblk_ref, tab_ref, d_ref, wglu_ref, o_ref,
               x_ref, e1_ref, e2_ref, st_ref):
    rows = h_ref.shape[0]

    @pl.when(pl.program_id(1) == 0)
    def _():
        st_ref[...] = jnp.zeros_like(st_ref)

    xn = _rms(h_ref[...], nw_ref[...]).astype(BF16)
    u = _dot(xn, win_ref[...])
    x_ref[...] = _dot(u.astype(BF16), bblk_ref[...])

    n1 = rows // SUBLANES
    n2 = n1 // SUBLANES
    _s5_scan_level(x_ref, n1, tab_ref, 0, e1_ref)
    _s5_scan_level(e1_ref, n2, tab_ref, 1, e2_ref)
    _s5_scan_level(e2_ref, 1, tab_ref, 2, None)
    _s5_fixup(e2_ref, 1, tab_ref, 2, None, st_ref)
    _s5_fixup(e1_ref, n2, tab_ref, 1, e2_ref, st_ref)
    _s5_fixup(x_ref, n1, tab_ref, 0, e1_ref, st_ref)
    st_ref[...] = jnp.broadcast_to(e2_ref[SUBLANES - 1:SUBLANES, :], st_ref.shape)

    y = _dot(x_ref[...].astype(BF16), cblk_ref[...]) + d_ref[...] * u
    g = jax.nn.gelu(y, approximate=True)
    z = _dot(g.astype(BF16), wglu_ref[...])
    o_ref[...] = (g * jax.nn.sigmoid(z)).astype(o_ref.dtype)


def _s5(h, bsz, seq, nw, win, bblk, cblk, tab, dskip, wglu):
    n, d = h.shape
    rows = S5_ROWS
    assert rows == SUBLANES ** 3 and seq % rows == 0
    nc = seq // rows
    consts = (nw, win, bblk, cblk, tab, dskip, wglu)
    return pl.pallas_call(
        _s5_kernel,
        grid=(bsz, nc),
        in_specs=[pl.BlockSpec((rows, d), lambda b, c: (b * nc + c, 0))]
        + [_const_spec(a, 2) for a in consts],
        out_specs=pl.BlockSpec((rows, S5_WIDTH), lambda b, c: (b * nc + c, 0)),
        out_shape=jax.ShapeDtypeStruct((n, S5_WIDTH), BF16),
        scratch_shapes=[pltpu.VMEM((rows, 2 * S5_NSTATE), F32),
                        pltpu.VMEM((rows // SUBLANES, 2 * S5_NSTATE), F32),
                        pltpu.VMEM((SUBLANES, 2 * S5_NSTATE), F32),
                        pltpu.VMEM((SUBLANES, 2 * S5_NSTATE), F32)],
        compiler_params=_cparams(48, 2),
        name="s5",
    )(h, *consts)


def _s5_params(lam_re, lam_im, log_step, b_re, b_im, c_re, c_im):
    lr = jnp.minimum(lam_re.astype(F32), -1e-4)
    li = lam_im.astype(F32)
    step = jnp.exp(log_step.astype(F32))[:, None]
    mag = jnp.exp(lr * step)
    ar = mag * jnp.cos(li * step)
    ai = mag * jnp.sin(li * step)
    den = lr * lr + li * li
    nr = ar - 1.0
    fr = (nr * lr + ai * li) / den
    fi = (ai * lr - nr * li) / den
    br = b_re.astype(F32)
    bi = b_im.astype(F32)
    bbr = fr[..., None] * br - fi[..., None] * bi
    bbi = fr[..., None] * bi + fi[..., None] * br
    eye = jnp.eye(S5_GROUPS, dtype=F32)

    def blockdiag(m):
        g, r, c = m.shape
        return (eye[:, None, :, None] * m[:, :, None, :]).reshape(g * r, g * c)

    bblk = jnp.concatenate([blockdiag(bbr.transpose(0, 2, 1)),
                            blockdiag(bbi.transpose(0, 2, 1))], axis=1)
    cblk = jnp.concatenate([blockdiag(c_re.astype(F32).transpose(0, 2, 1)),
                            blockdiag(-c_im.astype(F32).transpose(0, 2, 1))], axis=0)

    def cmul(x, y):
        return (x[0] * y[0] - x[1] * y[1], x[0] * y[1] + x[1] * y[0])

    a1 = (ar.reshape(-1), ai.reshape(-1))
    levels = []
    base = a1
    rowid = jnp.arange(SUBLANES)[:, None]
    for _ in range(3):
        p2 = cmul(base, base)
        p4 = cmul(p2, p2)
        carry = [base]
        for _i in range(SUBLANES - 1):
            carry.append(cmul(carry[-1], base))
        p8 = carry[-1]
        tabs = []
        for dshift, pw in ((1, base), (2, p2), (4, p4)):
            msk = (rowid >= dshift).astype(F32)
            tabs.append(jnp.concatenate([msk * pw[0][None, :], msk * pw[1][None, :]], axis=1))
        tabs.append(jnp.concatenate([jnp.stack([c[0] for c in carry]),
                                     jnp.stack([c[1] for c in carry])], axis=1))
        levels.append(jnp.stack(tabs))
        base = p8
    tab = jnp.stack(levels)
    return bblk.astype(BF16), cblk.astype(BF16), tab


def _gla_consts():
    n = GLA_CHUNK
    g = np.zeros((8, n, n), np.float32)
    valid = np.zeros((7, n, n), np.float32)
    rows = np.arange(n)
    for lev in range(6):
        s = n >> lev
        half = s // 2
        start = rows - rows % s
        bd = start + half - 1
        lower = (rows % s) < half
        for r in range(n):
            if lower[r]:
                g[lev, r, r + 1:bd[r] + 1] = 1.0
            else:
                g[lev, r, bd[r] + 1:r + 1] = 1.0
        same = start[:, None] == start[None, :]
        valid[lev] = same & (~lower)[:, None] & lower[None, :]
    valid[6] = np.eye(n)
    g[6] = np.tril(np.ones((n, n)))
    g[7] = rows[None, :] > rows[:, None]
    hm = np.zeros((GLA_HEADS, 1, GLA_KEY), np.float32)
    for h in range(GLA_HEADS):
        hm[h, 0, h * GLA_DK:(h + 1) * GLA_DK] = 1.0
    return g.reshape(8 * n, n), valid, hm


def _gla_kernel(h_ref, nw_ref, wq_ref, wk_ref, wv_ref, wgo_ref, wlr_ref, w2_ref, b2_ref, gn_ref,
                gmat_ref, valid_ref, hm_ref, o_ref,
                q_s, k_s, v_s, go_s, la_s, st_s):
    rows = h_ref.shape[0]
    n = GLA_CHUNK

    @pl.when(pl.program_id(1) == 0)
    def _():
        st_s[...] = jnp.zeros_like(st_s)

    xn = _rms(h_ref[...], nw_ref[...]).astype(BF16)
    q_s[...] = _dot(xn, wq_ref[...]) * (GLA_DK ** -0.5)
    k_s[...] = _dot(xn, wk_ref[...])
    v_s[...] = _dot(xn, wv_ref[...])
    go_s[...] = _dot(xn, wgo_ref[...])
    lr = _dot(xn, wlr_ref[...])
    xg = _dot(lr.astype(BF16), w2_ref[...]) + b2_ref[...]
    la_s[...] = _log_sigmoid(xg) * (1.0 / GLA_GATE_NORM)

    def chunk(ci, carry):
        r0 = pl.multiple_of(ci * n, n)
        la_hi, la_lo = _split_bf16(la_s[pl.ds(r0, n), :])
        e = jnp.exp(_dot(gmat_ref[...], la_hi) + _dot(gmat_ref[...], la_lo))
        q = q_s[pl.ds(r0, n), :]
        k = k_s[pl.ds(r0, n), :]
        e_cum = e[6 * n:7 * n]
        e_end = e[7 * n:8 * n]
        st_prev = st_s[...]
        st_prev_b = st_prev.astype(BF16)

        att = [jnp.zeros((n, n), F32) for _ in range(GLA_HEADS)]
        for lev in range(7):
            if lev < 6:
                el = e[lev * n:(lev + 1) * n]
                ql = q * el
                kl = (k * el).astype(BF16)
            else:
                ql = q
                kl = k.astype(BF16)
            for h in range(GLA_HEADS):
                sc = _dot_nt((ql * hm_ref[h]).astype(BF16), kl)
                att[h] = att[h] + valid_ref[lev] * sc

        qe = q * e_cum
        for h in range(GLA_HEADS):
            vh = v_s[pl.ds(r0, n), h * GLA_DV:(h + 1) * GLA_DV].astype(BF16)
            o = _dot(att[h].astype(BF16), vh)
            o = o + _dot_nt((qe * hm_ref[h]).astype(BF16), st_prev_b)
            o = _rms(o, gn_ref[...])
            go = go_s[pl.ds(r0, n), h * GLA_DV:(h + 1) * GLA_DV]
            o_ref[pl.ds(r0, n), h * GLA_DV:(h + 1) * GLA_DV] = (o * _silu(go)).astype(o_ref.dtype)

        khat = (k * e_end).astype(BF16)
        st = st_prev * e_cum[n - 1:n, :]
        for h in range(GLA_HEADS):
            vh = v_s[pl.ds(r0, n), h * GLA_DV:(h + 1) * GLA_DV].astype(BF16)
            st = st + hm_ref[h] * _dot_tn(vh, khat)
        st_s[...] = st
        return carry

    lax.fori_loop(0, rows // n, chunk, 0)


def _gla(h, bsz, seq, nw, wq, wk, wv, wgo, wlr, w2, b2, gn):
    n, d = h.shape
    rows = GLA_ROWS
    assert seq % rows == 0 and rows % GLA_CHUNK == 0
    nc = seq // rows
    gmat, valid, hm = _gla_consts()
    consts = (nw, wq, wk, wv, wgo, wlr, w2, b2, gn,
              jnp.asarray(gmat, BF16), jnp.asarray(valid), jnp.asarray(hm))
    return pl.pallas_call(
        _gla_kernel,
        grid=(bsz, nc),
        in_specs=[pl.BlockSpec((rows, d), lambda b, c: (b * nc + c, 0))]
        + [_const_spec(a, 2) for a in consts],
        out_specs=pl.BlockSpec((rows, GLA_VAL), lambda b, c: (b * nc + c, 0)),
        out_shape=jax.ShapeDtypeStruct((n, GLA_VAL), BF16),
        scratch_shapes=[pltpu.VMEM((rows, GLA_KEY), F32),
                        pltpu.VMEM((rows, GLA_KEY), F32),
                        pltpu.VMEM((rows, GLA_VAL), F32),
                        pltpu.VMEM((rows, GLA_VAL), F32),
                        pltpu.VMEM((rows, GLA_KEY), F32),
                        pltpu.VMEM((GLA_DV, GLA_KEY), F32)],
        compiler_params=_cparams(32, 2),
        name="gla",
    )(h, *consts)


def _ssd_kernel(h_ref, nw_ref, wz_ref, wx_ref, wdc_ref, wdr_ref, wde_ref,
                cw_ref, cb_ref, dbc_ref, dbr_ref, dbe_ref, alc_ref, alr_ref, ale_ref,
                de_ref, gn_ref, tril_ref, triu_ref, hmask_ref, o_ref,
                buf_s, xbc_s, z_s, dte_s, lae_s, lac_s, lar_s, st_s):
    rows = h_ref.shape[0]
    n = SSD_CHUNK
    tail = SUBLANES
    gw = SSD_INNER // SSD_GROUPS

    @pl.when(pl.program_id(1) == 0)
    def _():
        st_s[...] = jnp.zeros_like(st_s)
        buf_s[0:tail, :] = jnp.zeros((tail, SSD_CONV_DIM), F32)

    xn = _rms(h_ref[...], nw_ref[...]).astype(BF16)
    z_s[...] = _dot(xn, wz_ref[...])
    buf_s[tail:tail + rows, :] = _dot(xn, wx_ref[...])
    conv = cb_ref[...]
    for j in range(SSD_CONV):
        conv = conv + cw_ref[j:j + 1, :] * buf_s[pl.ds(tail - (SSD_CONV - 1) + j, rows), :]
    xbc_s[...] = _silu(conv)
    buf_s[0:tail, :] = buf_s[rows:rows + tail, :]

    dtc = _softplus(_dot(xn, wdc_ref[...]) + dbc_ref[...])
    lac_s[...] = dtc * (-jnp.exp(alc_ref[...]))
    dtr = _softplus(_dot_nt(wdr_ref[...], xn) + dbr_ref[...])
    lar_s[...] = dtr * (-jnp.exp(alr_ref[...]))
    dte = _softplus(_dot(xn, wde_ref[...]) + dbe_ref[...])
    dte_s[...] = dte
    lae_s[...] = dte * (-jnp.exp(ale_ref[...]))

    causal = lax.broadcasted_iota(jnp.int32, (n, n), 0) >= lax.broadcasted_iota(jnp.int32, (n, n), 1)

    def chunk(ci, carry):
        r0 = pl.multiple_of(ci * n, n)
        tril = tril_ref[...]
        hi, lo = _split_bf16(lac_s[pl.ds(r0, n), :])
        cum_c = _dot(tril, hi) + _dot(tril, lo)
        hi, lo = _split_bf16(lar_s[:, pl.ds(r0, n)])
        cum_r = _dot(hi, triu_ref[...]) + _dot(lo, triu_ref[...])
        hi, lo = _split_bf16(lae_s[pl.ds(r0, n), :])
        cum_e = _dot(tril, hi) + _dot(tril, lo)
        e_cum = jnp.exp(cum_e)
        c_last = cum_e[n - 1:n, :]
        d_state = jnp.exp(c_last - cum_e)
        xs = xbc_s[pl.ds(r0, n), 0:SSD_INNER]
        xdt = xs * dte_s[pl.ds(r0, n), :]
        xw = xdt * d_state
        z = z_s[pl.ds(r0, n), :]
        st_prev = st_s[...]
        st_decay = jnp.exp(c_last)

        for g in range(SSD_GROUPS):
            bm = xbc_s[pl.ds(r0, n), SSD_INNER + g * SSD_STATE:SSD_INNER + (g + 1) * SSD_STATE].astype(BF16)
            cm = xbc_s[pl.ds(r0, n), SSD_INNER + (SSD_GROUPS + g) * SSD_STATE:
                       SSD_INNER + (SSD_GROUPS + g + 1) * SSD_STATE].astype(BF16)
            sc = _dot_nt(cm, bm)
            y_off = _dot(cm, st_prev[:, g * gw:(g + 1) * gw].astype(BF16)) * e_cum[:, g * gw:(g + 1) * gw]
            for j in range(2):
                l0 = g * gw + j * LANES
                xp = xdt[:, l0:l0 + LANES]
                yd = jnp.zeros((n, LANES), F32)
                for s in range(2):
                    hd = g * 4 + j * 2 + s
                    diff = cum_c[:, hd:hd + 1] - cum_r[hd:hd + 1, :]
                    dec = jnp.exp(jnp.where(causal, diff, -jnp.inf))
                    yd = yd + _dot((sc * dec).astype(BF16), (xp * hmask_ref[s]).astype(BF16))
                y = yd + y_off[:, j * LANES:(j + 1) * LANES] + de_ref[:, l0:l0 + LANES] * xs[:, l0:l0 + LANES]
                y = y * _silu(z[:, l0:l0 + LANES])
                xbc_s[pl.ds(r0, n), l0:l0 + LANES] = y
            st_s[:, g * gw:(g + 1) * gw] = (st_prev[:, g * gw:(g + 1) * gw] * st_decay[:, g * gw:(g + 1) * gw]
                                            + _dot_tn(bm, xw[:, g * gw:(g + 1) * gw].astype(BF16)))
        for g in range(SSD_GROUPS):
            y = xbc_s[pl.ds(r0, n), g * gw:(g + 1) * gw]
            o_ref[pl.ds(r0, n), g * gw:(g + 1) * gw] = _rms(y, gn_ref[:, g * gw:(g + 1) * gw]).astype(o_ref.dtype)
        return carry

    lax.fori_loop(0, rows // n, chunk, 0)


def _ssd(h, bsz, seq, nw, wz, wx, wdc, wdr, wde, cw, cb, dbc, dbr, dbe, alc, alr, ale, de, gn):
    n, d = h.shape
    rows = SSD_ROWS
    assert seq % rows == 0 and rows % SSD_CHUNK == 0
    nc = seq // rows
    tril = np.tril(np.ones((SSD_CHUNK, SSD_CHUNK), np.float32))
    hmask = np.zeros((2, 1, LANES), np.float32)
    hmask[0, 0, :SSD_HEADDIM] = 1.0
    hmask[1, 0, SSD_HEADDIM:] = 1.0
    consts = (nw, wz, wx, wdc, wdr, wde, cw, cb, dbc, dbr, dbe, alc, alr, ale, de, gn,
              jnp.asarray(tril, BF16), jnp.asarray(tril.T, BF16), jnp.asarray(hmask))
    return pl.pallas_call(
        _ssd_kernel,
        grid=(bsz, nc),
        in_specs=[pl.BlockSpec((rows, d), lambda b, c: (b * nc + c, 0))]
        + [_const_spec(a, 2) for a in consts],
        out_specs=pl.BlockSpec((rows, SSD_INNER), lambda b, c: (b * nc + c, 0)),
        out_shape=jax.ShapeDtypeStruct((n, SSD_INNER), BF16),
        scratch_shapes=[pltpu.VMEM((rows + 2 * SUBLANES, SSD_CONV_DIM), F32),
                        pltpu.VMEM((rows, SSD_CONV_DIM), F32),
                        pltpu.VMEM((rows, SSD_INNER), F32),
                        pltpu.VMEM((rows, SSD_INNER), F32),
                        pltpu.VMEM((rows, SSD_INNER), F32),
                        pltpu.VMEM((rows, LANES), F32),
                        pltpu.VMEM((SUBLANES, rows), F32),
                        pltpu.VMEM((SSD_STATE, SSD_INNER), F32)],
        compiler_params=_cparams(32, 2),
        name="ssd",
    )(h, *consts)


def _merge_kernel(h_ref, nw_ref, wgate_ref, y0_ref, y1_ref, y2_ref, w0_ref, w1_ref, w2_ref,
                  wout_ref, o_ref):
    x = h_ref[...]
    xn = _rms(x, nw_ref[...]).astype(BF16)
    d = x.shape[1]
    merged = jnp.zeros_like(x)
    for i, (y_ref, w_ref) in enumerate(((y0_ref, w0_ref), (y1_ref, w1_ref), (y2_ref, w2_ref))):
        gate = jax.nn.sigmoid(_dot(xn, wgate_ref[:, i * d:(i + 1) * d]))
        merged = merged + gate * _dot(y_ref[...], w_ref[...])
    o_ref[...] = x + _dot(merged.astype(BF16), wout_ref[...])


def _merge(h, nw, wgate, ys, wbr, wout):
    n, d = h.shape
    rows = min(MERGE_ROWS, n)
    row_spec = pl.BlockSpec((rows, d), lambda i: (i, 0))
    y_specs = [pl.BlockSpec((rows, y.shape[1]), lambda i: (i, 0)) for y in ys]
    return pl.pallas_call(
        _merge_kernel,
        grid=(n // rows,),
        in_specs=[row_spec, _const_spec(nw, 1), _const_spec(wgate, 1)] + y_specs
        + [_const_spec(w, 1) for w in wbr] + [_const_spec(wout, 1)],
        out_specs=row_spec,
        out_shape=jax.ShapeDtypeStruct((n, d), F32),
        compiler_params=_cparams(40, 1),
        name="merge",
    )(h, nw, wgate, *ys, *wbr, wout)


def _ple_kernel(h_ref, nw_ref, wpg_ref, p_ref, wpp_ref, fn_ref, o_ref, *, final):
    x = h_ref[...]
    xn = _rms(x, nw_ref[...]).astype(BF16)
    gate = jax.nn.sigmoid(_dot(xn, wpg_ref[...]))
    y = x + gate * _dot(p_ref[...].astype(BF16), wpp_ref[...])
    if final:
        y = _rms(y, fn_ref[...])
    o_ref[...] = y


def _ple(h, nw, wpg, p, wpp, fn, final):
    n, d = h.shape
    rows = min(PLE_ROWS, n)
    row_spec = pl.BlockSpec((rows, d), lambda i: (i, 0))
    return pl.pallas_call(
        functools.partial(_ple_kernel, final=final),
        grid=(n // rows,),
        in_specs=[row_spec, _const_spec(nw, 1), _const_spec(wpg, 1),
                  pl.BlockSpec((rows, p.shape[1]), lambda i: (i, 0)), _const_spec(wpp, 1),
                  _const_spec(fn, 1)],
        out_specs=row_spec,
        out_shape=jax.ShapeDtypeStruct((n, d), F32),
        compiler_params=_cparams(32, 1),
        name="ple",
    )(h, nw, wpg, p, wpp, fn)


def _row(v):
    return v.astype(F32).reshape(1, -1)


def _pad_cols(w, width):
    return jnp.pad(w, ((0, 0), (0, width - w.shape[1])))


def _pad_rows(w, height):
    return jnp.pad(w, ((0, height - w.shape[0]), (0, 0)))


def _layer(h, p_i, bsz, seq, lp):
    (ffn1_norm, ffn1_gate, ffn1_up, ffn1_down, mix_norm, w_in,
     s5_lam_re, s5_lam_im, s5_log_step, s5_b_re, s5_b_im, s5_c_re, s5_c_im, s5_d, s5_glu,
     gla_gate_w2, gla_gate_b2, gla_norm, ssd_conv_w, ssd_conv_b, ssd_dt_bias, ssd_a_log,
     ssd_d, ssd_norm, w_br_s5, w_br_gla, w_br_ssd, w_out, ffn2_norm, ffn2_gate, ffn2_up,
     ffn2_down, ple_norm, ple_gate, ple_proj, final_norm, is_last) = lp

    def ffn_weights(wg, wu, wd):
        return (_pad_cols(wg, D_FF_PAD).astype(BF16), _pad_cols(wu, D_FF_PAD).astype(BF16),
                _pad_rows(wd, D_FF_PAD).astype(BF16))

    h = _ffn(h, _row(ffn1_norm), *ffn_weights(ffn1_gate, ffn1_up, ffn1_down))

    o = _IN_OFFS
    wcol = lambda i: w_in[:, o[i]:o[i + 1]]
    mixn = _row(mix_norm)

    bblk, cblk, tab = _s5_params(s5_lam_re, s5_lam_im, s5_log_step, s5_b_re, s5_b_im, s5_c_re, s5_c_im)
    y_s5 = _s5(h, bsz, seq, mixn, wcol(0).astype(BF16), bblk, cblk, tab, _row(s5_d),
               s5_glu.astype(BF16))

    y_gla = _gla(h, bsz, seq, mixn, wcol(1).astype(BF16), wcol(2).astype(BF16), wcol(3).astype(BF16),
                 wcol(4).astype(BF16), _pad_cols(wcol(5), LANES).astype(BF16),
                 _pad_rows(gla_gate_w2, LANES).astype(BF16), _row(gla_gate_b2), _row(gla_norm))

    w_dt = wcol(8)
    rep = SSD_HEADDIM
    y_ssd = _ssd(h, bsz, seq, mixn, wcol(6).astype(BF16), wcol(7).astype(BF16),
                 _pad_cols(w_dt, LANES).astype(BF16), w_dt.T.astype(BF16),
                 jnp.repeat(w_dt, rep, axis=1).astype(BF16),
                 ssd_conv_w.astype(F32), _row(ssd_conv_b),
                 _pad_cols(_row(ssd_dt_bias), LANES), ssd_dt_bias.astype(F32).reshape(-1, 1),
                 _row(jnp.repeat(ssd_dt_bias, rep)),
                 _pad_cols(_row(ssd_a_log), LANES), ssd_a_log.astype(F32).reshape(-1, 1),
                 _row(jnp.repeat(ssd_a_log, rep)),
                 _row(jnp.repeat(ssd_d, rep)), _row(ssd_norm))

    wgate = w_in[:, o[9]:o[12]].astype(BF16)
    h = _merge(h, mixn, wgate, (y_s5, y_gla, y_ssd),
               (w_br_s5.astype(BF16), w_br_gla.astype(BF16), w_br_ssd.astype(BF16)),
               w_out.astype(BF16))

    h = _ffn(h, _row(ffn2_norm), *ffn_weights(ffn2_gate, ffn2_up, ffn2_down))
    h = _ple(h, _row(ple_norm), ple_gate.astype(BF16), p_i, ple_proj.astype(BF16),
             _row(final_norm), is_last)
    return h


def kernel(x, p, ffn1_norm, ffn1_gate, ffn1_up, ffn1_down, mix_norm, w_in, s5_lam_re, s5_lam_im, s5_log_step, s5_b_re, s5_b_im, s5_c_re, s5_c_im, s5_d, s5_glu, gla_gate_w2, gla_gate_b2, gla_norm, ssd_conv_w, ssd_conv_b, ssd_dt_bias, ssd_a_log, ssd_d, ssd_norm, w_br_s5, w_br_gla, w_br_ssd, w_out, ffn2_norm, ffn2_gate, ffn2_up, ffn2_down, ple_norm, ple_gate, ple_proj, final_norm):
    bsz, seq, d = x.shape
    depth = p.shape[0]
    per_layer = (ffn1_norm, ffn1_gate, ffn1_up, ffn1_down, mix_norm, w_in,
                 s5_lam_re, s5_lam_im, s5_log_step, s5_b_re, s5_b_im, s5_c_re, s5_c_im, s5_d, s5_glu,
                 gla_gate_w2, gla_gate_b2, gla_norm, ssd_conv_w, ssd_conv_b, ssd_dt_bias, ssd_a_log,
                 ssd_d, ssd_norm, w_br_s5, w_br_gla, w_br_ssd, w_out, ffn2_norm, ffn2_gate, ffn2_up,
                 ffn2_down, ple_norm, ple_gate, ple_proj)
    h = x.reshape(bsz * seq, d)
    for i in range(depth):
        lp = tuple(a[i] for a in per_layer) + (final_norm, i == depth - 1)
        h = _layer(h, p[i].reshape(bsz * seq, -1), bsz, seq, lp)
    return h.reshape(bsz, seq, d)
```

```python
import functools

import numpy as np
import jax
import jax.numpy as jnp
from jax import lax
from jax.experimental import pallas as pl
from jax.experimental.pallas import tpu as pltpu

F32 = jnp.float32
BF16 = jnp.bfloat16

EPS = 1e-6
D_MODEL = 1024
D_FF = 2752
PLE_DIM = 256

S5_WIDTH = 512
S5_GROUP = 16
S5_GROUPS = 32
S5_STATE = 64
S5_NSTATE = S5_GROUPS * S5_STATE

GLA_HEADS = 4
GLA_DK = 64
GLA_DV = 128
GLA_KEY = 256
GLA_VAL = 512
GLA_RANK = 16
GLA_GATE_NORM = 16.0

SSD_HEADS = 8
SSD_HEADDIM = 64
SSD_INNER = 512
SSD_GROUPS = 2
SSD_STATE = 128
SSD_CONV = 4
SSD_CONV_DIM = 1024

_IN_SIZES = (S5_WIDTH, GLA_KEY, GLA_KEY, GLA_VAL, GLA_VAL, GLA_RANK,
             SSD_INNER, SSD_CONV_DIM, SSD_HEADS, D_MODEL, D_MODEL, D_MODEL)
_IN_OFFS = tuple(int(v) for v in np.concatenate([[0], np.cumsum(_IN_SIZES)]))

LANES = 128
SUBLANES = 8
VMEM_BYTES_V7X = 64 * 1024 * 1024

FFN_ROWS = 512
FFN_SPLIT = 2
D_FF_PAD = 2816
S5_ROWS = 512
GLA_ROWS = 256
GLA_CHUNK = 64
SSD_ROWS = 256
SSD_CHUNK = 128
MERGE_ROWS = 512
PLE_ROWS = 512


def _cparams(vmem_mib, ndims):
    return pltpu.CompilerParams(
        dimension_semantics=("arbitrary",) * ndims,
        vmem_limit_bytes=vmem_mib * 1024 * 1024)


def _const_spec(arr, ngrid):
    zeros = (0,) * arr.ndim
    if ngrid == 1:
        imap = lambda i: zeros
    else:
        imap = lambda b, c: zeros
    return pl.BlockSpec(arr.shape, imap, pipeline_mode=pl.Buffered(1))


def _rms(x, w):
    ms = jnp.mean(x * x, axis=-1, keepdims=True)
    return x * lax.rsqrt(ms + EPS) * w


def _silu(x):
    return x * jax.nn.sigmoid(x)


def _softplus(x):
    return jnp.maximum(x, 0.0) + jnp.log1p(jnp.exp(-jnp.abs(x)))


def _log_sigmoid(x):
    return jnp.minimum(x, 0.0) - jnp.log1p(jnp.exp(-jnp.abs(x)))


def _dot(a, b):
    return jnp.dot(a, b, preferred_element_type=F32)


def _dot_nt(a, b):
    return lax.dot_general(a, b, (((1,), (1,)), ((), ())), preferred_element_type=F32)


def _dot_tn(a, b):
    return lax.dot_general(a, b, (((0,), (0,)), ((), ())), preferred_element_type=F32)


def _split_bf16(x):
    hi = x.astype(BF16)
    lo = (x - hi.astype(F32)).astype(BF16)
    return hi, lo


def _ffn_kernel(h_ref, nw_ref, wg_ref, wu_ref, wd_ref, *rest, emit_norm):
    x = h_ref[...]
    xn = _rms(x, nw_ref[...]).astype(BF16)
    fc = wg_ref.shape[1] // FFN_SPLIT
    acc = jnp.zeros_like(x)
    for c in range(FFN_SPLIT):
        g = _dot(xn, wg_ref[:, c * fc:(c + 1) * fc])
        u = _dot(xn, wu_ref[:, c * fc:(c + 1) * fc])
        hid = (_silu(g) * u).astype(BF16)
        acc = acc + _dot(hid, wd_ref[c * fc:(c + 1) * fc, :])
    y = x + 0.5 * acc
    if emit_norm:
        nw2_ref, o_ref, u_ref = rest
        u_ref[...] = _rms(y, nw2_ref[...]).astype(u_ref.dtype)
    else:
        (o_ref,) = rest
    o_ref[...] = y


def _ffn(h, nw, wg, wu, wd, nw2=None):
    n, d = h.shape
    rows = min(FFN_ROWS, n)
    row_spec = pl.BlockSpec((rows, d), lambda i: (i, 0))
    emit_norm = nw2 is not None
    consts = (nw, wg, wu, wd) + ((nw2,) if emit_norm else ())
    out_shape = jax.ShapeDtypeStruct((n, d), F32)
    return pl.pallas_call(
        functools.partial(_ffn_kernel, emit_norm=emit_norm),
        grid=(n // rows,),
        in_specs=[row_spec] + [_const_spec(a, 1) for a in consts],
        out_specs=(row_spec, row_spec) if emit_norm else row_spec,
        out_shape=(out_shape, jax.ShapeDtypeStruct((n, d), BF16)) if emit_norm else out_shape,
        compiler_params=_cparams(54, 1),
        name="ffn",
    )(h, *consts)


def _s5_scan_level(r_ref, nblk, tab_ref, lev, e_ref):
    p = S5_NSTATE

    def body(k, carry):
        r0 = pl.multiple_of(k * SUBLANES, SUBLANES)
        xr = r_ref[pl.ds(r0, SUBLANES), 0:p]
        xi = r_ref[pl.ds(r0, SUBLANES), p:2 * p]
        for j, d in enumerate((1, 2, 4)):
            pr = tab_ref[lev, j, :, 0:p]
            pi = tab_ref[lev, j, :, p:2 * p]
            sr = pltpu.roll(xr, d, 0)
            si = pltpu.roll(xi, d, 0)
            xr, xi = xr + pr * sr - pi * si, xi + pr * si + pi * sr
        r_ref[pl.ds(r0, SUBLANES), 0:p] = xr
        r_ref[pl.ds(r0, SUBLANES), p:2 * p] = xi
        if e_ref is not None:
            e_ref[pl.ds(k, 1), 0:p] = xr[SUBLANES - 1:SUBLANES, :]
            e_ref[pl.ds(k, 1), p:2 * p] = xi[SUBLANES - 1:SUBLANES, :]
        return carry

    lax.fori_loop(0, nblk, body, 0)


def _s5_fixup(r_ref, nblk, tab_ref, lev, prev_ref, s_ref):
    p = S5_NSTATE

    def apply(k, pr, pi):
        r0 = pl.multiple_of(k * SUBLANES, SUBLANES)
        cr = tab_ref[lev, 3, :, 0:p]
        ci = tab_ref[lev, 3, :, p:2 * p]
        xr = r_ref[pl.ds(r0, SUBLANES), 0:p]
        xi = r_ref[pl.ds(r0, SUBLANES), p:2 * p]
        r_ref[pl.ds(r0, SUBLANES), 0:p] = xr + cr * pr - ci * pi
        r_ref[pl.ds(r0, SUBLANES), p:2 * p] = xi + cr * pi + ci * pr

    apply(0, s_ref[0:1, 0:p], s_ref[0:1, p:2 * p])

    def body(k, carry):
        apply(k, prev_ref[pl.ds(k - 1, 1), 0:p], prev_ref[pl.ds(k - 1, 1), p:2 * p])
        return carry

    if nblk > 1:
        lax.fori_loop(1, nblk, body, 0)


def _s5_kernel(xn_ref, win_ref, bblk_ref, cblk_ref, tab_ref, d_ref, wglu_ref, o_ref,
               x_ref, e1_ref, e2_ref, st_ref):
    rows = xn_ref.shape[0]

    @pl.when(pl.program_id(1) == 0)
    def _():
        st_ref[...] = jnp.zeros_like(st_ref)

    u = _dot(xn_ref[...], win_ref[...])
    ub = u.astype(BF16)
    hw = S5_WIDTH // 2
    hp = S5_NSTATE // 2
    for j in range(2):
        xj = _dot(ub[:, j * hw:(j + 1) * hw], bblk_ref[j])
        x_ref[:, j * hp:(j + 1) * hp] = xj[:, 0:hp]
        x_ref[:, S5_NSTATE + j * hp:S5_NSTATE + (j + 1) * hp] = xj[:, hp:2 * hp]

    n1 = rows // SUBLANES
    n2 = n1 // SUBLANES
    _s5_scan_level(x_ref, n1, tab_ref, 0, e1_ref)
    _s5_scan_level(e1_ref, n2, tab_ref, 1, e2_ref)
    _s5_scan_level(e2_ref, 1, tab_ref, 2, None)
    _s5_fixup(e2_ref, 1, tab_ref, 2, None, st_ref)
    _s5_fixup(e1_ref, n2, tab_ref, 1, e2_ref, st_ref)
    _s5_fixup(x_ref, n1, tab_ref, 0, e1_ref, st_ref)
    st_ref[...] = jnp.broadcast_to(e2_ref[SUBLANES - 1:SUBLANES, :], st_ref.shape)

    ys = []
    for j in range(2):
        hr = x_ref[:, j * hp:(j + 1) * hp].astype(BF16)
        hi = x_ref[:, S5_NSTATE + j * hp:S5_NSTATE + (j + 1) * hp].astype(BF16)
        ys.append(_dot(hr, cblk_ref[j, 0:hp, :]) + _dot(hi, cblk_ref[j, hp:2 * hp, :]))
    y = jnp.concatenate(ys, axis=1) + d_ref[...] * u
    g = jax.nn.gelu(y, approximate=True)
    z = _dot(g.astype(BF16), wglu_ref[...])
    o_ref[...] = (g * jax.nn.sigmoid(z)).astype(o_ref.dtype)


def _s5(h, bsz, seq, win, bblk, cblk, tab, dskip, wglu):
    n, d = h.shape
    rows = S5_ROWS
    assert rows == SUBLANES ** 3 and seq % rows == 0
    nc = seq // rows
    consts = (win, bblk, cblk, tab, dskip, wglu)
    return pl.pallas_call(
        _s5_kernel,
        grid=(bsz, nc),
        in_specs=[pl.BlockSpec((rows, d), lambda b, c: (b * nc + c, 0))]
        + [_const_spec(a, 2) for a in consts],
        out_specs=pl.BlockSpec((rows, S5_WIDTH), lambda b, c: (b * nc + c, 0)),
        out_shape=jax.ShapeDtypeStruct((n, S5_WIDTH), BF16),
        scratch_shapes=[pltpu.VMEM((rows, 2 * S5_NSTATE), F32),
                        pltpu.VMEM((rows // SUBLANES, 2 * S5_NSTATE), F32),
                        pltpu.VMEM((SUBLANES, 2 * S5_NSTATE), F32),
                        pltpu.VMEM((SUBLANES, 2 * S5_NSTATE), F32)],
        compiler_params=_cparams(48, 2),
        name="s5",
    )(h, *consts)


def _s5_params(lam_re, lam_im, log_step, b_re, b_im, c_re, c_im):
    lr = jnp.minimum(lam_re.astype(F32), -1e-4)
    li = lam_im.astype(F32)
    step = jnp.exp(log_step.astype(F32))[:, None]
    mag = jnp.exp(lr * step)
    ar = mag * jnp.cos(li * step)
    ai = mag * jnp.sin(li * step)
    den = lr * lr + li * li
    nr = ar - 1.0
    fr = (nr * lr + ai * li) / den
    fi = (ai * lr - nr * li) / den
    br = b_re.astype(F32)
    bi = b_im.astype(F32)
    bbr = fr[..., None] * br - fi[..., None] * bi
    bbi = fr[..., None] * bi + fi[..., None] * br
    eye = jnp.eye(S5_GROUPS, dtype=F32)

    def blockdiag(m):
        g, r, c = m.shape
        return (eye[:, None, :, None] * m[:, :, None, :]).reshape(g * r, g * c)

    hw = S5_WIDTH // 2
    hp = S5_NSTATE // 2
    b_r = blockdiag(bbr.transpose(0, 2, 1))
    b_i = blockdiag(bbi.transpose(0, 2, 1))
    c_r = blockdiag(c_re.astype(F32).transpose(0, 2, 1))
    c_i = blockdiag(-c_im.astype(F32).transpose(0, 2, 1))
    bblk = jnp.stack([jnp.concatenate([b_r[j * hw:(j + 1) * hw, j * hp:(j + 1) * hp],
                                       b_i[j * hw:(j + 1) * hw, j * hp:(j + 1) * hp]], axis=1)
                      for j in range(2)])
    cblk = jnp.stack([jnp.concatenate([c_r[j * hp:(j + 1) * hp, j * hw:(j + 1) * hw],
                                       c_i[j * hp:(j + 1) * hp, j * hw:(j + 1) * hw]], axis=0)
                      for j in range(2)])

    def cmul(x, y):
        return (x[0] * y[0] - x[1] * y[1], x[0] * y[1] + x[1] * y[0])

    a1 = (ar.reshape(-1), ai.reshape(-1))
    levels = []
    base = a1
    rowid = jnp.arange(SUBLANES)[:, None]
    for _ in range(3):
        p2 = cmul(base, base)
        p4 = cmul(p2, p2)
        carry = [base]
        for _i in range(SUBLANES - 1):
            carry.append(cmul(carry[-1], base))
        p8 = carry[-1]
        tabs = []
        for dshift, pw in ((1, base), (2, p2), (4, p4)):
            msk = (rowid >= dshift).astype(F32)
            tabs.append(jnp.concatenate([msk * pw[0][None, :], msk * pw[1][None, :]], axis=1))
        tabs.append(jnp.concatenate([jnp.stack([c[0] for c in carry]),
                                     jnp.stack([c[1] for c in carry])], axis=1))
        levels.append(jnp.stack(tabs))
        base = p8
    tab = jnp.stack(levels)
    return bblk.astype(BF16), cblk.astype(BF16), tab


def _gla_consts():
    n = GLA_CHUNK
    g = np.zeros((8, n, n), np.float32)
    valid = np.zeros((7, n, n), np.float32)
    rows = np.arange(n)
    for lev in range(6):
        s = n >> lev
        half = s // 2
        start = rows - rows % s
        bd = start + half - 1
        lower = (rows % s) < half
        for r in range(n):
            if lower[r]:
                g[lev, r, r + 1:bd[r] + 1] = 1.0
            else:
                g[lev, r, bd[r] + 1:r + 1] = 1.0
        same = start[:, None] == start[None, :]
        valid[lev] = same & (~lower)[:, None] & lower[None, :]
    valid[6] = np.eye(n)
    g[6] = np.tril(np.ones((n, n)))
    g[7] = rows[None, :] > rows[:, None]
    g = g.reshape(8 * n, n)
    g2 = np.concatenate([g, g], axis=1)
    valid_t = np.tile(valid, (1, 1, GLA_HEADS))
    heads = np.arange(GLA_HEADS)
    eye_h = heads[:, None] == heads[None, :]
    kmask = np.kron(eye_h, np.ones((n, GLA_DK)))
    vmask = np.kron(eye_h, np.ones((n, GLA_DV)))
    smask = np.kron(eye_h, np.ones((GLA_DV, GLA_DK)))
    return g2, valid_t, kmask, vmask, smask


def _gla_kernel(xn_ref, wq_ref, wk_ref, wv_ref, wgo_ref, wlr_ref, w2_ref, b2_ref, gn_ref,
                g2_ref, valid_ref, kmask_ref, vmask_ref, smask_ref, o_ref,
                q_s, k_s, v_s, go_s, la_s, st_s):
    rows = xn_ref.shape[0]
    n = GLA_CHUNK

    @pl.when(pl.program_id(1) == 0)
    def _():
        st_s[...] = jnp.zeros_like(st_s)

    xn = xn_ref[...]
    q_s[...] = _dot(xn, wq_ref[...]) * (GLA_DK ** -0.5)
    k_s[...] = _dot(xn, wk_ref[...])
    v_s[...] = _dot(xn, wv_ref[...]).astype(BF16)
    go_s[...] = _dot(xn, wgo_ref[...])
    lr = _dot(xn, wlr_ref[...])
    xg = _dot(lr.astype(BF16), w2_ref[...]) + b2_ref[...]
    la_s[...] = _log_sigmoid(xg) * (1.0 / GLA_GATE_NORM)

    def chunk(ci):
        r0 = ci * n
        la_hi, la_lo = _split_bf16(la_s[pl.ds(r0, n), :])
        e = jnp.exp(_dot(g2_ref[...], jnp.concatenate([la_hi, la_lo], axis=0)))
        q = q_s[pl.ds(r0, n), :]
        k = k_s[pl.ds(r0, n), :]
        v = v_s[pl.ds(r0, n), :]
        e_cum = e[6 * n:7 * n]
        e_end = e[7 * n:8 * n]
        st_prev = st_s[...]

        att = jnp.zeros((n, GLA_HEADS * n), F32)
        for lev in range(7):
            if lev < 6:
                el = e[lev * n:(lev + 1) * n]
                ql = (q * el).astype(BF16)
                kl = (k * el).astype(BF16)
            else:
                ql = q.astype(BF16)
                kl = k.astype(BF16)
            kblk = jnp.concatenate([kl] * GLA_HEADS, axis=0) * kmask_ref[...]
            att = att + valid_ref[lev] * _dot_nt(ql, kblk)

        vblk = jnp.concatenate([v] * GLA_HEADS, axis=0) * vmask_ref[...]
        o = _dot(att.astype(BF16), vblk) + _dot_nt((q * e_cum).astype(BF16), st_prev.astype(BF16))
        for h in range(GLA_HEADS):
            sl = slice(h * GLA_DV, (h + 1) * GLA_DV)
            oh = _rms(o[:, sl], gn_ref[...])
            o_ref[pl.ds(r0, n), sl] = (oh * _silu(go_s[pl.ds(r0, n), sl])).astype(o_ref.dtype)

        khat = (k * e_end).astype(BF16)
        st_s[...] = st_prev * e_cum[n - 1:n, :] + smask_ref[...] * _dot_tn(v, khat)

    for ci in range(rows // n):
        chunk(ci)


def _gla(h, bsz, seq, wq, wk, wv, wgo, wlr, w2, b2, gn):
    n, d = h.shape
    rows = GLA_ROWS
    assert seq % rows == 0 and rows % GLA_CHUNK == 0
    nc = seq // rows
    g2, valid_t, kmask, vmask, smask = _gla_consts()
    consts = (wq, wk, wv, wgo, wlr, w2, b2, gn,
              jnp.asarray(g2, BF16), jnp.asarray(valid_t, F32), jnp.asarray(kmask, BF16),
              jnp.asarray(vmask, BF16), jnp.asarray(smask, F32))
    return pl.pallas_call(
        _gla_kernel,
        grid=(bsz, nc),
        in_specs=[pl.BlockSpec((rows, d), lambda b, c: (b * nc + c, 0))]
        + [_const_spec(a, 2) for a in consts],
        out_specs=pl.BlockSpec((rows, GLA_VAL), lambda b, c: (b * nc + c, 0)),
        out_shape=jax.ShapeDtypeStruct((n, GLA_VAL), BF16),
        scratch_shapes=[pltpu.VMEM((rows, GLA_KEY), F32),
                        pltpu.VMEM((rows, GLA_KEY), F32),
                        pltpu.VMEM((rows, GLA_VAL), BF16),
                        pltpu.VMEM((rows, GLA_VAL), F32),
                        pltpu.VMEM((rows, GLA_KEY), F32),
                        pltpu.VMEM((GLA_HEADS * GLA_DV, GLA_KEY), F32)],
        compiler_params=_cparams(32, 2),
        name="gla",
    )(h, *consts)


def _ssd_kernel(xn_ref, wz_ref, wx_ref, wdc_ref, wdr_ref,
                cw_ref, cb_ref, dbc_ref, dbr_ref, alc_ref, alr_ref, ale_ref,
                de_ref, gn_ref, tril_ref, triu_ref, hmask_ref, expand_ref, o_ref,
                buf_s, xbc_s, z_s, dte_s, lae_s, lac_s, lar_s, st_s):
    rows = xn_ref.shape[0]
    n = SSD_CHUNK
    tail = SUBLANES
    gw = SSD_INNER // SSD_GROUPS
    nslab = SSD_CONV_DIM // LANES

    @pl.when(pl.program_id(1) == 0)
    def _():
        st_s[...] = jnp.zeros_like(st_s)
        buf_s[:, 0:tail, :] = jnp.zeros((nslab, tail, LANES), F32)

    xn = xn_ref[...]
    z_s[...] = _dot(xn, wz_ref[...])
    xpre = _dot(xn, wx_ref[...])
    for j in range(nslab):
        buf_s[j, tail:tail + rows, :] = xpre[:, j * LANES:(j + 1) * LANES]
    for j in range(nslab):
        sl = slice(j * LANES, (j + 1) * LANES)
        conv = cb_ref[:, sl]
        for t in range(SSD_CONV):
            conv = conv + cw_ref[t:t + 1, sl] * buf_s[j, pl.ds(tail - (SSD_CONV - 1) + t, rows), :]
        xbc_s[:, sl] = _silu(conv)
    buf_s[:, 0:tail, :] = buf_s[:, rows:rows + tail, :]

    dtc = _softplus(_dot(xn, wdc_ref[...]) + dbc_ref[...])
    lac_s[...] = dtc * (-jnp.exp(alc_ref[...]))
    dtr = _softplus(_dot_nt(wdr_ref[...], xn) + dbr_ref[...])
    lar_s[...] = dtr * (-jnp.exp(alr_ref[...]))
    d_hi = dtc.astype(BF16)
    r1 = dtc - d_hi.astype(F32)
    d_mid = r1.astype(BF16)
    d_lo = (r1 - d_mid.astype(F32)).astype(BF16)
    dte = (_dot(d_hi, expand_ref[...]) + _dot(d_mid, expand_ref[...])) + _dot(d_lo, expand_ref[...])
    dte_s[...] = dte
    lae_s[...] = dte * (-jnp.exp(ale_ref[...]))

    causal = lax.broadcasted_iota(jnp.int32, (n, n), 0) >= lax.broadcasted_iota(jnp.int32, (n, n), 1)

    def chunk(ci):
        r0 = ci * n
        tril = tril_ref[...]
        hi, lo = _split_bf16(lac_s[pl.ds(r0, n), :])
        cum_c = _dot(tril, hi) + _dot(tril, lo)
        hi, lo = _split_bf16(lar_s[:, pl.ds(r0, n)])
        cum_r = _dot(hi, triu_ref[...]) + _dot(lo, triu_ref[...])
        hi, lo = _split_bf16(lae_s[pl.ds(r0, n), :])
        cum_e = _dot(tril, hi) + _dot(tril, lo)
        e_cum = jnp.exp(cum_e)
        c_last = cum_e[n - 1:n, :]
        d_state = jnp.exp(c_last - cum_e)
        xs = xbc_s[pl.ds(r0, n), 0:SSD_INNER]
        xdt = xs * dte_s[pl.ds(r0, n), :]
        xw = xdt * d_state
        z = z_s[pl.ds(r0, n), :]
        st_prev = st_s[...]
        st_decay = jnp.exp(c_last)

        for g in range(SSD_GROUPS):
            bm = xbc_s[pl.ds(r0, n), SSD_INNER + g * SSD_STATE:SSD_INNER + (g + 1) * SSD_STATE].astype(BF16)
            cm = xbc_s[pl.ds(r0, n), SSD_INNER + (SSD_GROUPS + g) * SSD_STATE:
                       SSD_INNER + (SSD_GROUPS + g + 1) * SSD_STATE].astype(BF16)
            sc = _dot_nt(cm, bm)
            y_off = _dot(cm, st_prev[:, g * gw:(g + 1) * gw].astype(BF16)) * e_cum[:, g * gw:(g + 1) * gw]
            for j in range(2):
                l0 = g * gw + j * LANES
                xp = xdt[:, l0:l0 + LANES]
                yd = jnp.zeros((n, LANES), F32)
                for s in range(2):
                    hd = g * 4 + j * 2 + s
                    diff = cum_c[:, hd:hd + 1] - cum_r[hd:hd + 1, :]
                    dec = jnp.exp(jnp.where(causal, diff, -jnp.inf))
                    yd = yd + _dot((sc * dec).astype(BF16), (xp * hmask_ref[s]).astype(BF16))
                y = yd + y_off[:, j * LANES:(j + 1) * LANES] + de_ref[:, l0:l0 + LANES] * xs[:, l0:l0 + LANES]
                y = y * _silu(z[:, l0:l0 + LANES])
                xbc_s[pl.ds(r0, n), l0:l0 + LANES] = y
            st_s[:, g * gw:(g + 1) * gw] = (st_prev[:, g * gw:(g + 1) * gw] * st_decay[:, g * gw:(g + 1) * gw]
                                            + _dot_tn(bm, xw[:, g * gw:(g + 1) * gw].astype(BF16)))
        for g in range(SSD_GROUPS):
            y = xbc_s[pl.ds(r0, n), g * gw:(g + 1) * gw]
            o_ref[pl.ds(r0, n), g * gw:(g + 1) * gw] = _rms(y, gn_ref[:, g * gw:(g + 1) * gw]).astype(o_ref.dtype)

    for ci in range(rows // n):
        chunk(ci)


def _ssd(h, bsz, seq, wz, wx, wdc, wdr, cw, cb, dbc, dbr, alc, alr, ale, de, gn):
    n, d = h.shape
    rows = SSD_ROWS
    assert seq % rows == 0 and rows % SSD_CHUNK == 0
    nc = seq // rows
    tril = np.tril(np.ones((SSD_CHUNK, SSD_CHUNK), np.float32))
    hmask = np.zeros((2, 1, LANES), np.float32)
    hmask[0, 0, :SSD_HEADDIM] = 1.0
    hmask[1, 0, SSD_HEADDIM:] = 1.0
    expand = np.zeros((LANES, SSD_INNER), np.float32)
    for hd in range(SSD_HEADS):
        expand[hd, hd * SSD_HEADDIM:(hd + 1) * SSD_HEADDIM] = 1.0
    consts = (wz, wx, wdc, wdr, cw, cb, dbc, dbr, alc, alr, ale, de, gn,
              jnp.asarray(tril, BF16), jnp.asarray(tril.T, BF16), jnp.asarray(hmask),
              jnp.asarray(expand, BF16))
    return pl.pallas_call(
        _ssd_kernel,
        grid=(bsz, nc),
        in_specs=[pl.BlockSpec((rows, d), lambda b, c: (b * nc + c, 0))]
        + [_const_spec(a, 2) for a in consts],
        out_specs=pl.BlockSpec((rows, SSD_INNER), lambda b, c: (b * nc + c, 0)),
        out_shape=jax.ShapeDtypeStruct((n, SSD_INNER), BF16),
        scratch_shapes=[pltpu.VMEM((SSD_CONV_DIM // LANES, rows + 2 * SUBLANES, LANES), F32),
                        pltpu.VMEM((rows, SSD_CONV_DIM), F32),
                        pltpu.VMEM((rows, SSD_INNER), F32),
                        pltpu.VMEM((rows, SSD_INNER), F32),
                        pltpu.VMEM((rows, SSD_INNER), F32),
                        pltpu.VMEM((rows, LANES), F32),
                        pltpu.VMEM((SUBLANES, rows), F32),
                        pltpu.VMEM((SSD_STATE, SSD_INNER), F32)],
        compiler_params=_cparams(32, 2),
        name="ssd",
    )(h, *consts)


def _merge_kernel(h_ref, xn_ref, wgate_ref, y0_ref, y1_ref, y2_ref, w0_ref, w1_ref, w2_ref,
                  wout_ref, o_ref):
    xn = xn_ref[...]
    d = h_ref.shape[1]
    merged = jnp.zeros(h_ref.shape, F32)
    for i, (y_ref, w_ref) in enumerate(((y0_ref, w0_ref), (y1_ref, w1_ref), (y2_ref, w2_ref))):
        gate = jax.nn.sigmoid(_dot(xn, wgate_ref[:, i * d:(i + 1) * d]))
        merged = merged + gate * _dot(y_ref[...], w_ref[...])
    o_ref[...] = h_ref[...] + _dot(merged.astype(BF16), wout_ref[...])


def _merge(h, xn, wgate, ys, wbr, wout):
    n, d = h.shape
    rows = min(MERGE_ROWS, n)
    row_spec = pl.BlockSpec((rows, d), lambda i: (i, 0))
    y_specs = [pl.BlockSpec((rows, y.shape[1]), lambda i: (i, 0)) for y in ys]
    return pl.pallas_call(
        _merge_kernel,
        grid=(n // rows,),
        in_specs=[row_spec, row_spec, _const_spec(wgate, 1)] + y_specs
        + [_const_spec(w, 1) for w in wbr] + [_const_spec(wout, 1)],
        out_specs=row_spec,
        out_shape=jax.ShapeDtypeStruct((n, d), F32),
        compiler_params=_cparams(40, 1),
        name="merge",
    )(h, xn, wgate, *ys, *wbr, wout)


def _ple_kernel(h_ref, nw_ref, wpg_ref, p_ref, wpp_ref, fn_ref, o_ref, *, final):
    x = h_ref[...]
    xn = _rms(x, nw_ref[...]).astype(BF16)
    gate = jax.nn.sigmoid(_dot(xn, wpg_ref[...]))
    y = x + gate * _dot(p_ref[...].astype(BF16), wpp_ref[...])
    if final:
        y = _rms(y, fn_ref[...])
    o_ref[...] = y


def _ple(h, nw, wpg, p, wpp, fn, final):
    n, d = h.shape
    rows = min(PLE_ROWS, n)
    row_spec = pl.BlockSpec((rows, d), lambda i: (i, 0))
    return pl.pallas_call(
        functools.partial(_ple_kernel, final=final),
        grid=(n // rows,),
        in_specs=[row_spec, _const_spec(nw, 1), _const_spec(wpg, 1),
                  pl.BlockSpec((rows, p.shape[1]), lambda i: (i, 0)), _const_spec(wpp, 1),
                  _const_spec(fn, 1)],
        out_specs=row_spec,
        out_shape=jax.ShapeDtypeStruct((n, d), F32),
        compiler_params=_cparams(32, 1),
        name="ple",
    )(h, nw, wpg, p, wpp, fn)


def _row(v):
    return v.astype(F32).reshape(1, -1)


def _pad_cols(w, width):
    return jnp.pad(w, ((0, 0), (0, width - w.shape[1])))


def _pad_rows(w, height):
    return jnp.pad(w, ((0, height - w.shape[0]), (0, 0)))


def _layer(h, p_i, bsz, seq, lp):
    (ffn1_norm, ffn1_gate, ffn1_up, ffn1_down, mix_norm, w_in,
     s5_lam_re, s5_lam_im, s5_log_step, s5_b_re, s5_b_im, s5_c_re, s5_c_im, s5_d, s5_glu,
     gla_gate_w2, gla_gate_b2, gla_norm, ssd_conv_w, ssd_conv_b, ssd_dt_bias, ssd_a_log,
     ssd_d, ssd_norm, w_br_s5, w_br_gla, w_br_ssd, w_out, ffn2_norm, ffn2_gate, ffn2_up,
     ffn2_down, ple_norm, ple_gate, ple_proj, final_norm, is_last) = lp

    def ffn_weights(wg, wu, wd):
        return (_pad_cols(wg, D_FF_PAD).astype(BF16), _pad_cols(wu, D_FF_PAD).astype(BF16),
                _pad_rows(wd, D_FF_PAD).astype(BF16))

    h, u = _ffn(h, _row(ffn1_norm), *ffn_weights(ffn1_gate, ffn1_up, ffn1_down), nw2=_row(mix_norm))

    o = _IN_OFFS
    wcol = lambda i: w_in[:, o[i]:o[i + 1]]

    bblk, cblk, tab = _s5_params(s5_lam_re, s5_lam_im, s5_log_step, s5_b_re, s5_b_im, s5_c_re, s5_c_im)
    y_s5 = _s5(u, bsz, seq, wcol(0).astype(BF16), bblk, cblk, tab, _row(s5_d),
               s5_glu.astype(BF16))

    y_gla = _gla(u, bsz, seq, wcol(1).astype(BF16), wcol(2).astype(BF16), wcol(3).astype(BF16),
                 wcol(4).astype(BF16), _pad_cols(wcol(5), LANES).astype(BF16),
                 _pad_rows(gla_gate_w2, LANES).astype(BF16), _row(gla_gate_b2), _row(gla_norm))

    w_dt = wcol(8)
    rep = SSD_HEADDIM
    y_ssd = _ssd(u, bsz, seq, wcol(6).astype(BF16), wcol(7).astype(BF16),
                 _pad_cols(w_dt, LANES).astype(BF16), w_dt.T.astype(BF16),
                 ssd_conv_w.astype(F32), _row(ssd_conv_b),
                 _pad_cols(_row(ssd_dt_bias), LANES), ssd_dt_bias.astype(F32).reshape(-1, 1),
                 _pad_cols(_row(ssd_a_log), LANES), ssd_a_log.astype(F32).reshape(-1, 1),
                 _row(jnp.repeat(ssd_a_log, rep)),
                 _row(jnp.repeat(ssd_d, rep)), _row(ssd_norm))

    wgate = w_in[:, o[9]:o[12]].astype(BF16)
    h = _merge(h, u, wgate, (y_s5, y_gla, y_ssd),
               (w_br_s5.astype(BF16), w_br_gla.astype(BF16), w_br_ssd.astype(BF16)),
               w_out.astype(BF16))

    h = _ffn(h, _row(ffn2_norm), *ffn_weights(ffn2_gate, ffn2_up, ffn2_down))
    h = _ple(h, _row(ple_norm), ple_gate.astype(BF16), p_i, ple_proj.astype(BF16),
             _row(final_norm), is_last)
    return h


def kernel(x, p, ffn1_norm, ffn1_gate, ffn1_up, ffn1_down, mix_norm, w_in, s5_lam_re, s5_lam_im, s5_log_step, s5_b_re, s5_b_im, s5_c_re, s5_c_im, s5_d, s5_glu, gla_gate_w2, gla_gate_b2, gla_norm, ssd_conv_w, ssd_conv_b, ssd_dt_bias, ssd_a_log, ssd_d, ssd_norm, w_br_s5, w_br_gla, w_br_ssd, w_out, ffn2_norm, ffn2_gate, ffn2_up, ffn2_down, ple_norm, ple_gate, ple_proj, final_norm):
    bsz, seq, d = x.shape
    depth = p.shape[0]
    per_layer = (ffn1_norm, ffn1_gate, ffn1_up, ffn1_down, mix_norm, w_in,
                 s5_lam_re, s5_lam_im, s5_log_step, s5_b_re, s5_b_im, s5_c_re, s5_c_im, s5_d, s5_glu,
                 gla_gate_w2, gla_gate_b2, gla_norm, ssd_conv_w, ssd_conv_b, ssd_dt_bias, ssd_a_log,
                 ssd_d, ssd_norm, w_br_s5, w_br_gla, w_br_ssd, w_out, ffn2_norm, ffn2_gate, ffn2_up,
                 ffn2_down, ple_norm, ple_gate, ple_proj)
    h = x.reshape(bsz * seq, d)
    for i in range(depth):
        lp = tuple(a[i] for a in per_layer) + (final_norm, i == depth - 1)
        h = _layer(h, p[i].reshape(bsz * seq, -1), bsz, seq, lp)
    return h.reshape(bsz, seq, d)
```

```python
import functools

import numpy as np
import jax
import jax.numpy as jnp
from jax import lax
from jax.experimental import pallas as pl
from jax.experimental.pallas import tpu as pltpu

F32 = jnp.float32
BF16 = jnp.bfloat16

EPS = 1e-6
D_MODEL = 1024
D_FF = 2752
PLE_DIM = 256

S5_WIDTH = 512
S5_GROUP = 16
S5_GROUPS = 32
S5_STATE = 64
S5_NSTATE = S5_GROUPS * S5_STATE

GLA_HEADS = 4
GLA_DK = 64
GLA_DV = 128
GLA_KEY = 256
GLA_VAL = 512
GLA_RANK = 16
GLA_GATE_NORM = 16.0

SSD_HEADS = 8
SSD_HEADDIM = 64
SSD_INNER = 512
SSD_GROUPS = 2
SSD_STATE = 128
SSD_CONV = 4
SSD_CONV_DIM = 1024

_IN_SIZES = (S5_WIDTH, GLA_KEY, GLA_KEY, GLA_VAL, GLA_VAL, GLA_RANK,
             SSD_INNER, SSD_CONV_DIM, SSD_HEADS, D_MODEL, D_MODEL, D_MODEL)
_IN_OFFS = tuple(int(v) for v in np.concatenate([[0], np.cumsum(_IN_SIZES)]))

_W_IN_PACK = (("gates", 3 * D_MODEL, 9, 12), ("xbc", SSD_CONV_DIM, 7, 8), ("s5", S5_WIDTH, 0, 1),
              ("v", GLA_VAL, 3, 4), ("go", GLA_VAL, 4, 5), ("z", SSD_INNER, 6, 7),
              ("q", GLA_KEY, 1, 2), ("k", GLA_KEY, 2, 3), ("lr", 128, 5, 6), ("dt", 128, 8, 9))
W_IN_COL = {}
_off = 0
for _name, _width, _s0, _s1 in _W_IN_PACK:
    assert _off % _width == 0
    W_IN_COL[_name] = _off // _width
    _off += _width
W_IN_PACKED = _off

LANES = 128
SUBLANES = 8
VMEM_BYTES_V7X = 64 * 1024 * 1024

FFN_ROWS = 512
FFN_SPLIT = 2
D_FF_PAD = 2816
S5_ROWS = 512
GLA_ROWS = 256
GLA_CHUNK = 64
SSD_ROWS = 256
SSD_CHUNK = 128
MERGE_ROWS = 512


def _cparams(vmem_mib, ndims):
    return pltpu.CompilerParams(
        dimension_semantics=("arbitrary",) * ndims,
        vmem_limit_bytes=vmem_mib * 1024 * 1024)


def _const_spec(arr, ngrid):
    zeros = (0,) * arr.ndim
    if ngrid == 1:
        imap = lambda i: zeros
    else:
        imap = lambda b, c: zeros
    return pl.BlockSpec(arr.shape, imap, pipeline_mode=pl.Buffered(1))


def _layer_spec(arr, layer, ngrid, width=None, col=0):
    shape = (None,) + arr.shape[1:]
    idx = (layer,) + (0,) * (arr.ndim - 1)
    if width is not None:
        shape = shape[:-1] + (width,)
        idx = idx[:-1] + (col,)
    if ngrid == 1:
        imap = lambda i: idx
    else:
        imap = lambda b, c: idx
    return pl.BlockSpec(shape, imap, pipeline_mode=pl.Buffered(1))


def _rms(x, w):
    ms = jnp.mean(x * x, axis=-1, keepdims=True)
    return x * lax.rsqrt(ms + EPS) * w


def _silu(x):
    return x * jax.nn.sigmoid(x)


def _softplus(x):
    return jnp.maximum(x, 0.0) + jnp.log1p(jnp.exp(-jnp.abs(x)))


def _log_sigmoid(x):
    return jnp.minimum(x, 0.0) - jnp.log1p(jnp.exp(-jnp.abs(x)))


def _dot(a, b):
    return jnp.dot(a, b, preferred_element_type=F32)


def _dot_nt(a, b):
    return lax.dot_general(a, b, (((1,), (1,)), ((), ())), preferred_element_type=F32)


def _dot_tn(a, b):
    return lax.dot_general(a, b, (((0,), (0,)), ((), ())), preferred_element_type=F32)


def _split_bf16(x):
    hi = x.astype(BF16)
    lo = (x - hi.astype(F32)).astype(BF16)
    return hi, lo


def _ffn_core(x, nw_ref, wg_ref, wu_ref, wd_ref):
    xn = _rms(x, nw_ref[...]).astype(BF16)
    fc = wg_ref.shape[1] // FFN_SPLIT
    acc = jnp.zeros_like(x)
    for c in range(FFN_SPLIT):
        g = _dot(xn, wg_ref[:, c * fc:(c + 1) * fc])
        u = _dot(xn, wu_ref[:, c * fc:(c + 1) * fc])
        hid = (_silu(g) * u).astype(BF16)
        acc = acc + _dot(hid, wd_ref[c * fc:(c + 1) * fc, :])
    return x + 0.5 * acc


def _ffn_norm_kernel(h_ref, nw_ref, wg_ref, wu_ref, wd_ref, nw2_ref, o_ref, u_ref):
    y = _ffn_core(h_ref[...], nw_ref, wg_ref, wu_ref, wd_ref)
    o_ref[...] = y
    u_ref[...] = _rms(y, nw2_ref[...]).astype(u_ref.dtype)


def _ffn_norm(h, layer, nw, wg, wu, wd, nw2):
    n, d = h.shape
    rows = min(FFN_ROWS, n)
    row_spec = pl.BlockSpec((rows, d), lambda i: (i, 0))
    ls = lambda a: _layer_spec(a, layer, 1)
    return pl.pallas_call(
        _ffn_norm_kernel,
        grid=(n // rows,),
        in_specs=[row_spec, ls(nw), ls(wg), ls(wu), ls(wd), ls(nw2)],
        out_specs=(row_spec, row_spec),
        out_shape=(jax.ShapeDtypeStruct((n, d), F32), jax.ShapeDtypeStruct((n, d), BF16)),
        compiler_params=_cparams(54, 1),
        name="ffn_norm",
    )(h, nw, wg, wu, wd, nw2)


def _s5_kernel(xn_ref, win_ref, bblk_ref, cblk_ref, tab_ref, d_ref, wglu_ref, o_ref,
               x_s, e1_s, e2_s, st_s):
    rows = xn_ref.shape[0]
    n1 = rows // SUBLANES
    n2 = n1 // SUBLANES

    @pl.when(pl.program_id(1) == 0)
    def _():
        st_s[...] = jnp.zeros_like(st_s)

    u = _s5_project(xn_ref[...], win_ref, bblk_ref, x_s)
    for k in range(n1):
        _s5_scan_block(x_s, k, tab_ref, 0, e1_s)
    for k in range(n2):
        _s5_scan_block(e1_s, k, tab_ref, 1, e2_s)
    _s5_scan_block(e2_s, 0, tab_ref, 2, None)
    _s5_fix_block(e2_s, 0, tab_ref, 2, st_s, 0)
    _s5_fix_block(e1_s, 0, tab_ref, 1, st_s, 0)
    for k in range(1, n2):
        _s5_fix_block(e1_s, k, tab_ref, 1, e2_s, k - 1)
    _s5_fix_block(x_s, 0, tab_ref, 0, st_s, 0)
    for k in range(1, n1):
        _s5_fix_block(x_s, k, tab_ref, 0, e1_s, k - 1)
    st_s[...] = jnp.broadcast_to(e2_s[SUBLANES - 1:SUBLANES, :], st_s.shape)
    o_ref[...] = _s5_output(u, cblk_ref, d_ref, wglu_ref, x_s).astype(o_ref.dtype)


def _s5(u, bsz, seq, layer, w_in, bblk, cblk, tab, dskip, wglu):
    n, d = u.shape
    rows = S5_ROWS
    assert rows == SUBLANES ** 3 and seq % rows == 0
    nc = seq // rows
    ls = lambda a: _layer_spec(a, layer, 2)
    return pl.pallas_call(
        _s5_kernel,
        grid=(bsz, nc),
        in_specs=[pl.BlockSpec((rows, d), lambda b, c: (b * nc + c, 0)),
                  _layer_spec(w_in, layer, 2, S5_WIDTH, W_IN_COL["s5"]),
                  ls(bblk), ls(cblk), ls(tab), ls(dskip), ls(wglu)],
        out_specs=pl.BlockSpec((rows, S5_WIDTH), lambda b, c: (b * nc + c, 0)),
        out_shape=jax.ShapeDtypeStruct((n, S5_WIDTH), BF16),
        scratch_shapes=[pltpu.VMEM((rows, 2 * S5_NSTATE), F32),
                        pltpu.VMEM((rows // SUBLANES, 2 * S5_NSTATE), F32),
                        pltpu.VMEM((SUBLANES, 2 * S5_NSTATE), F32),
                        pltpu.VMEM((SUBLANES, 2 * S5_NSTATE), F32)],
        compiler_params=_cparams(48, 2),
        name="s5",
    )(u, w_in, bblk, cblk, tab, dskip, wglu)


def _ffn_ple_kernel(h_ref, nw_ref, wg_ref, wu_ref, wd_ref, pn_ref, wpg_ref, p_ref, wpp_ref, fn_ref,
                    o_ref, *, final):
    y = _ffn_core(h_ref[...], nw_ref, wg_ref, wu_ref, wd_ref)
    gate = jax.nn.sigmoid(_dot(_rms(y, pn_ref[...]).astype(BF16), wpg_ref[...]))
    y = y + gate * _dot(p_ref[...].astype(BF16), wpp_ref[...])
    if final:
        y = _rms(y, fn_ref[...])
    o_ref[...] = y


def _ffn_ple(h, layer, nw, wg, wu, wd, pn, wpg, p, wpp, fn, final):
    n, d = h.shape
    rows = min(FFN_ROWS, n)
    row_spec = pl.BlockSpec((rows, d), lambda i: (i, 0))
    ls = lambda a: _layer_spec(a, layer, 1)
    return pl.pallas_call(
        functools.partial(_ffn_ple_kernel, final=final),
        grid=(n // rows,),
        in_specs=[row_spec, ls(nw), ls(wg), ls(wu), ls(wd), ls(pn), ls(wpg),
                  pl.BlockSpec((None, rows, p.shape[2]), lambda i: (layer, i, 0)), ls(wpp),
                  _const_spec(fn, 1)],
        out_specs=row_spec,
        out_shape=jax.ShapeDtypeStruct((n, d), F32),
        compiler_params=_cparams(56, 1),
        name="ffn_ple",
    )(h, nw, wg, wu, wd, pn, wpg, p, wpp, fn)


def _block_row0(k):
    return k * SUBLANES if isinstance(k, int) else pl.multiple_of(k * SUBLANES, SUBLANES)


def _s5_scan_block(r_ref, k, tab_ref, lev, e_ref):
    p = S5_NSTATE
    r0 = _block_row0(k)
    xr = r_ref[pl.ds(r0, SUBLANES), 0:p]
    xi = r_ref[pl.ds(r0, SUBLANES), p:2 * p]
    for j, d in enumerate((1, 2, 4)):
        pr = tab_ref[lev, j, :, 0:p]
        pi = tab_ref[lev, j, :, p:2 * p]
        sr = pltpu.roll(xr, d, 0)
        si = pltpu.roll(xi, d, 0)
        xr, xi = xr + pr * sr - pi * si, xi + pr * si + pi * sr
    r_ref[pl.ds(r0, SUBLANES), 0:p] = xr
    r_ref[pl.ds(r0, SUBLANES), p:2 * p] = xi
    if e_ref is not None:
        e_ref[pl.ds(k, 1), 0:p] = xr[SUBLANES - 1:SUBLANES, :]
        e_ref[pl.ds(k, 1), p:2 * p] = xi[SUBLANES - 1:SUBLANES, :]


def _s5_fix_block(r_ref, k, tab_ref, lev, src_ref, row):
    p = S5_NSTATE
    pr = src_ref[pl.ds(row, 1), 0:p]
    pi = src_ref[pl.ds(row, 1), p:2 * p]
    r0 = _block_row0(k)
    cr = tab_ref[lev, 3, :, 0:p]
    ci = tab_ref[lev, 3, :, p:2 * p]
    xr = r_ref[pl.ds(r0, SUBLANES), 0:p]
    xi = r_ref[pl.ds(r0, SUBLANES), p:2 * p]
    r_ref[pl.ds(r0, SUBLANES), 0:p] = xr + cr * pr - ci * pi
    r_ref[pl.ds(r0, SUBLANES), p:2 * p] = xi + cr * pi + ci * pr


def _s5_project(xn, win_ref, bblk_ref, x_ref):
    u = _dot(xn, win_ref[...])
    ub = u.astype(BF16)
    hw = S5_WIDTH // 2
    hp = S5_NSTATE // 2
    for j in range(2):
        xj = _dot(ub[:, j * hw:(j + 1) * hw], bblk_ref[j])
        x_ref[:, j * hp:(j + 1) * hp] = xj[:, 0:hp]
        x_ref[:, S5_NSTATE + j * hp:S5_NSTATE + (j + 1) * hp] = xj[:, hp:2 * hp]
    return u


def _s5_output(u, cblk_ref, d_ref, wglu_ref, x_ref):
    hp = S5_NSTATE // 2
    ys = []
    for j in range(2):
        hr = x_ref[:, j * hp:(j + 1) * hp].astype(BF16)
        hi = x_ref[:, S5_NSTATE + j * hp:S5_NSTATE + (j + 1) * hp].astype(BF16)
        ys.append(_dot(hr, cblk_ref[j, 0:hp, :]) + _dot(hi, cblk_ref[j, hp:2 * hp, :]))
    y = jnp.concatenate(ys, axis=1) + d_ref[...] * u
    g = jax.nn.gelu(y, approximate=True)
    z = _dot(g.astype(BF16), wglu_ref[...])
    return g * jax.nn.sigmoid(z)


def _s5_params(lam_re, lam_im, log_step, b_re, b_im, c_re, c_im):
    lr = jnp.minimum(lam_re.astype(F32), -1e-4)
    li = lam_im.astype(F32)
    step = jnp.exp(log_step.astype(F32))[:, None]
    mag = jnp.exp(lr * step)
    ar = mag * jnp.cos(li * step)
    ai = mag * jnp.sin(li * step)
    den = lr * lr + li * li
    nr = ar - 1.0
    fr = (nr * lr + ai * li) / den
    fi = (ai * lr - nr * li) / den
    br = b_re.astype(F32)
    bi = b_im.astype(F32)
    bbr = fr[..., None] * br - fi[..., None] * bi
    bbi = fr[..., None] * bi + fi[..., None] * br
    eye = jnp.eye(S5_GROUPS, dtype=F32)

    def blockdiag(m):
        g, r, c = m.shape
        return (eye[:, None, :, None] * m[:, :, None, :]).reshape(g * r, g * c)

    hw = S5_WIDTH // 2
    hp = S5_NSTATE // 2
    b_r = blockdiag(bbr.transpose(0, 2, 1))
    b_i = blockdiag(bbi.transpose(0, 2, 1))
    c_r = blockdiag(c_re.astype(F32).transpose(0, 2, 1))
    c_i = blockdiag(-c_im.astype(F32).transpose(0, 2, 1))
    bblk = jnp.stack([jnp.concatenate([b_r[j * hw:(j + 1) * hw, j * hp:(j + 1) * hp],
                                       b_i[j * hw:(j + 1) * hw, j * hp:(j + 1) * hp]], axis=1)
                      for j in range(2)])
    cblk = jnp.stack([jnp.concatenate([c_r[j * hp:(j + 1) * hp, j * hw:(j + 1) * hw],
                                       c_i[j * hp:(j + 1) * hp, j * hw:(j + 1) * hw]], axis=0)
                      for j in range(2)])

    def cmul(x, y):
        return (x[0] * y[0] - x[1] * y[1], x[0] * y[1] + x[1] * y[0])

    a1 = (ar.reshape(-1), ai.reshape(-1))
    levels = []
    base = a1
    rowid = jnp.arange(SUBLANES)[:, None]
    for _ in range(3):
        p2 = cmul(base, base)
        p4 = cmul(p2, p2)
        carry = [base]
        for _i in range(SUBLANES - 1):
            carry.append(cmul(carry[-1], base))
        p8 = carry[-1]
        tabs = []
        for dshift, pw in ((1, base), (2, p2), (4, p4)):
            msk = (rowid >= dshift).astype(F32)
            tabs.append(jnp.concatenate([msk * pw[0][None, :], msk * pw[1][None, :]], axis=1))
        tabs.append(jnp.concatenate([jnp.stack([c[0] for c in carry]),
                                     jnp.stack([c[1] for c in carry])], axis=1))
        levels.append(jnp.stack(tabs))
        base = p8
    tab = jnp.stack(levels)
    return bblk.astype(BF16), cblk.astype(BF16), tab


def _gla_consts():
    n = GLA_CHUNK
    g = np.zeros((8, n, n), np.float32)
    valid = np.zeros((7, n, n), np.float32)
    rows = np.arange(n)
    for lev in range(6):
        s = n >> lev
        half = s // 2
        start = rows - rows % s
        bd = start + half - 1
        lower = (rows % s) < half
        for r in range(n):
            if lower[r]:
                g[lev, r, r + 1:bd[r] + 1] = 1.0
            else:
                g[lev, r, bd[r] + 1:r + 1] = 1.0
        same = start[:, None] == start[None, :]
        valid[lev] = same & (~lower)[:, None] & lower[None, :]
    valid[6] = np.eye(n)
    g[6] = np.tril(np.ones((n, n)))
    g[7] = rows[None, :] > rows[:, None]
    g = g.reshape(8 * n, n)
    g2 = np.concatenate([g, g], axis=1)
    valid_t = np.tile(valid, (1, 1, GLA_HEADS))
    heads = np.arange(GLA_HEADS)
    eye_h = heads[:, None] == heads[None, :]
    kmask = np.kron(eye_h, np.ones((n, GLA_DK)))
    vmask = np.kron(eye_h, np.ones((n, GLA_DV)))
    smask = np.kron(eye_h, np.ones((GLA_DV, GLA_DK)))
    return g2, valid_t, kmask, vmask, smask


def _gla_kernel(xn_ref, wq_ref, wk_ref, wv_ref, wgo_ref, wlr_ref, w2_ref, b2_ref, gn_ref,
                g2_ref, valid_ref, kmask_ref, vmask_ref, smask_ref, o_ref,
                q_s, k_s, v_s, go_s, la_s, st_s):
    rows = xn_ref.shape[0]
    n = GLA_CHUNK

    @pl.when(pl.program_id(1) == 0)
    def _():
        st_s[...] = jnp.zeros_like(st_s)

    xn = xn_ref[...]
    q_s[...] = _dot(xn, wq_ref[...]) * (GLA_DK ** -0.5)
    k_s[...] = _dot(xn, wk_ref[...])
    v_s[...] = _dot(xn, wv_ref[...]).astype(BF16)
    go_s[...] = _dot(xn, wgo_ref[...])
    lr = _dot(xn, wlr_ref[...])
    xg = _dot(lr.astype(BF16), w2_ref[...]) + b2_ref[...]
    la_s[...] = _log_sigmoid(xg) * (1.0 / GLA_GATE_NORM)

    def chunk(ci):
        r0 = ci * n
        la_hi, la_lo = _split_bf16(la_s[pl.ds(r0, n), :])
        e = jnp.exp(_dot(g2_ref[...], jnp.concatenate([la_hi, la_lo], axis=0)))
        q = q_s[pl.ds(r0, n), :]
        k = k_s[pl.ds(r0, n), :]
        v = v_s[pl.ds(r0, n), :]
        e_cum = e[6 * n:7 * n]
        e_end = e[7 * n:8 * n]
        st_prev = st_s[...]

        att = jnp.zeros((n, GLA_HEADS * n), F32)
        for lev in range(7):
            if lev < 6:
                el = e[lev * n:(lev + 1) * n]
                ql = (q * el).astype(BF16)
                kl = (k * el).astype(BF16)
            else:
                ql = q.astype(BF16)
                kl = k.astype(BF16)
            kblk = jnp.concatenate([kl] * GLA_HEADS, axis=0) * kmask_ref[...]
            att = att + valid_ref[lev] * _dot_nt(ql, kblk)

        vblk = jnp.concatenate([v] * GLA_HEADS, axis=0) * vmask_ref[...]
        o = _dot(att.astype(BF16), vblk) + _dot_nt((q * e_cum).astype(BF16), st_prev.astype(BF16))
        for h in range(GLA_HEADS):
            sl = slice(h * GLA_DV, (h + 1) * GLA_DV)
            oh = _rms(o[:, sl], gn_ref[...])
            o_ref[pl.ds(r0, n), sl] = (oh * _silu(go_s[pl.ds(r0, n), sl])).astype(o_ref.dtype)

        khat = (k * e_end).astype(BF16)
        st_s[...] = st_prev * e_cum[n - 1:n, :] + smask_ref[...] * _dot_tn(v, khat)

    for ci in range(rows // n):
        chunk(ci)


def _gla(u, bsz, seq, layer, w_in, w2, b2, gn):
    n, d = u.shape
    rows = GLA_ROWS
    assert seq % rows == 0 and rows % GLA_CHUNK == 0
    nc = seq // rows
    g2, valid_t, kmask, vmask, smask = _gla_consts()
    consts = (jnp.asarray(g2, BF16), jnp.asarray(valid_t, F32), jnp.asarray(kmask, BF16),
              jnp.asarray(vmask, BF16), jnp.asarray(smask, F32))
    wspec = lambda name, width: _layer_spec(w_in, layer, 2, width, W_IN_COL[name])
    ls = lambda a: _layer_spec(a, layer, 2)
    return pl.pallas_call(
        _gla_kernel,
        grid=(bsz, nc),
        in_specs=[pl.BlockSpec((rows, d), lambda b, c: (b * nc + c, 0)),
                  wspec("q", GLA_KEY), wspec("k", GLA_KEY), wspec("v", GLA_VAL), wspec("go", GLA_VAL),
                  wspec("lr", LANES), ls(w2), ls(b2), ls(gn)]
        + [_const_spec(a, 2) for a in consts],
        out_specs=pl.BlockSpec((rows, GLA_VAL), lambda b, c: (b * nc + c, 0)),
        out_shape=jax.ShapeDtypeStruct((n, GLA_VAL), BF16),
        scratch_shapes=[pltpu.VMEM((rows, GLA_KEY), F32),
                        pltpu.VMEM((rows, GLA_KEY), F32),
                        pltpu.VMEM((rows, GLA_VAL), BF16),
                        pltpu.VMEM((rows, GLA_VAL), F32),
                        pltpu.VMEM((rows, GLA_KEY), F32),
                        pltpu.VMEM((GLA_HEADS * GLA_DV, GLA_KEY), F32)],
        compiler_params=_cparams(32, 2),
        name="gla",
    )(u, w_in, w_in, w_in, w_in, w_in, w2, b2, gn, *consts)


def _ssd_kernel(xn_ref, wz_ref, wx_ref, wdc_ref, wdr_ref,
                cw_ref, cb_ref, dbc_ref, dbr_ref, alc_ref, alr_ref, ale_ref,
                de_ref, gn_ref, tril_ref, triu_ref, hmask_ref, expand_ref, o_ref,
                buf_s, xbc_s, z_s, dte_s, lae_s, lac_s, lar_s, st_s):
    rows = xn_ref.shape[0]
    n = SSD_CHUNK
    tail = SUBLANES
    gw = SSD_INNER // SSD_GROUPS
    nslab = SSD_CONV_DIM // LANES

    @pl.when(pl.program_id(1) == 0)
    def _():
        st_s[...] = jnp.zeros_like(st_s)
        buf_s[:, 0:tail, :] = jnp.zeros((nslab, tail, LANES), F32)

    xn = xn_ref[...]
    z_s[...] = _dot(xn, wz_ref[...])
    xpre = _dot(xn, wx_ref[...])
    for j in range(nslab):
        buf_s[j, tail:tail + rows, :] = xpre[:, j * LANES:(j + 1) * LANES]
    for j in range(nslab):
        sl = slice(j * LANES, (j + 1) * LANES)
        conv = cb_ref[:, sl]
        for t in range(SSD_CONV):
            conv = conv + cw_ref[t:t + 1, sl] * buf_s[j, pl.ds(tail - (SSD_CONV - 1) + t, rows), :]
        xbc_s[:, sl] = _silu(conv)
    buf_s[:, 0:tail, :] = buf_s[:, rows:rows + tail, :]

    dtc = _softplus(_dot(xn, wdc_ref[...]) + dbc_ref[...])
    lac_s[...] = dtc * (-jnp.exp(alc_ref[...]))
    dtr = _softplus(_dot_nt(wdr_ref[...], xn) + dbr_ref[...])
    lar_s[...] = dtr * (-jnp.exp(alr_ref[...]))
    d_hi = dtc.astype(BF16)
    r1 = dtc - d_hi.astype(F32)
    d_mid = r1.astype(BF16)
    d_lo = (r1 - d_mid.astype(F32)).astype(BF16)
    dte = (_dot(d_hi, expand_ref[...]) + _dot(d_mid, expand_ref[...])) + _dot(d_lo, expand_ref[...])
    dte_s[...] = dte
    lae_s[...] = dte * (-jnp.exp(ale_ref[...]))

    causal = lax.broadcasted_iota(jnp.int32, (n, n), 0) >= lax.broadcasted_iota(jnp.int32, (n, n), 1)

    def chunk(ci):
        r0 = ci * n
        tril = tril_ref[...]
        hi, lo = _split_bf16(lac_s[pl.ds(r0, n), :])
        cum_c = _dot(tril, hi) + _dot(tril, lo)
        hi, lo = _split_bf16(lar_s[:, pl.ds(r0, n)])
        cum_r = _dot(hi, triu_ref[...]) + _dot(lo, triu_ref[...])
        hi, lo = _split_bf16(lae_s[pl.ds(r0, n), :])
        cum_e = _dot(tril, hi) + _dot(tril, lo)
        e_cum = jnp.exp(cum_e)
        c_last = cum_e[n - 1:n, :]
        d_state = jnp.exp(c_last - cum_e)
        xs = xbc_s[pl.ds(r0, n), 0:SSD_INNER]
        xdt = xs * dte_s[pl.ds(r0, n), :]
        xw = xdt * d_state
        z = z_s[pl.ds(r0, n), :]
        st_prev = st_s[...]
        st_decay = jnp.exp(c_last)

        for g in range(SSD_GROUPS):
            bm = xbc_s[pl.ds(r0, n), SSD_INNER + g * SSD_STATE:SSD_INNER + (g + 1) * SSD_STATE].astype(BF16)
            cm = xbc_s[pl.ds(r0, n), SSD_INNER + (SSD_GROUPS + g) * SSD_STATE:
                       SSD_INNER + (SSD_GROUPS + g + 1) * SSD_STATE].astype(BF16)
            sc = _dot_nt(cm, bm)
            y_off = _dot(cm, st_prev[:, g * gw:(g + 1) * gw].astype(BF16)) * e_cum[:, g * gw:(g + 1) * gw]
            for j in range(2):
                l0 = g * gw + j * LANES
                xp = xdt[:, l0:l0 + LANES]
                yd = jnp.zeros((n, LANES), F32)
                for s in range(2):
                    hd = g * 4 + j * 2 + s
                    diff = cum_c[:, hd:hd + 1] - cum_r[hd:hd + 1, :]
                    dec = jnp.exp(jnp.where(causal, diff, -jnp.inf))
                    yd = yd + _dot((sc * dec).astype(BF16), (xp * hmask_ref[s]).astype(BF16))
                y = yd + y_off[:, j * LANES:(j + 1) * LANES] + de_ref[:, l0:l0 + LANES] * xs[:, l0:l0 + LANES]
                y = y * _silu(z[:, l0:l0 + LANES])
                xbc_s[pl.ds(r0, n), l0:l0 + LANES] = y
            st_s[:, g * gw:(g + 1) * gw] = (st_prev[:, g * gw:(g + 1) * gw] * st_decay[:, g * gw:(g + 1) * gw]
                                            + _dot_tn(bm, xw[:, g * gw:(g + 1) * gw].astype(BF16)))
        for g in range(SSD_GROUPS):
            y = xbc_s[pl.ds(r0, n), g * gw:(g + 1) * gw]
            o_ref[pl.ds(r0, n), g * gw:(g + 1) * gw] = _rms(y, gn_ref[:, g * gw:(g + 1) * gw]).astype(o_ref.dtype)

    for ci in range(rows // n):
        chunk(ci)


def _ssd(u, bsz, seq, layer, w_in, wdr, cw, cb, dbc, dbr, alc, alr, ale, de, gn):
    n, d = u.shape
    rows = SSD_ROWS
    assert seq % rows == 0 and rows % SSD_CHUNK == 0
    nc = seq // rows
    tril = np.tril(np.ones((SSD_CHUNK, SSD_CHUNK), np.float32))
    hmask = np.zeros((2, 1, LANES), np.float32)
    hmask[0, 0, :SSD_HEADDIM] = 1.0
    hmask[1, 0, SSD_HEADDIM:] = 1.0
    expand = np.zeros((LANES, SSD_INNER), np.float32)
    for hd in range(SSD_HEADS):
        expand[hd, hd * SSD_HEADDIM:(hd + 1) * SSD_HEADDIM] = 1.0
    params = (wdr, cw, cb, dbc, dbr, alc, alr, ale, de, gn)
    consts = (jnp.asarray(tril, BF16), jnp.asarray(tril.T, BF16), jnp.asarray(hmask),
              jnp.asarray(expand, BF16))
    wspec = lambda name, width: _layer_spec(w_in, layer, 2, width, W_IN_COL[name])
    return pl.pallas_call(
        _ssd_kernel,
        grid=(bsz, nc),
        in_specs=[pl.BlockSpec((rows, d), lambda b, c: (b * nc + c, 0)),
                  wspec("z", SSD_INNER), wspec("xbc", SSD_CONV_DIM), wspec("dt", LANES)]
        + [_layer_spec(a, layer, 2) for a in params] + [_const_spec(a, 2) for a in consts],
        out_specs=pl.BlockSpec((rows, SSD_INNER), lambda b, c: (b * nc + c, 0)),
        out_shape=jax.ShapeDtypeStruct((n, SSD_INNER), BF16),
        scratch_shapes=[pltpu.VMEM((SSD_CONV_DIM // LANES, rows + 2 * SUBLANES, LANES), F32),
                        pltpu.VMEM((rows, SSD_CONV_DIM), F32),
                        pltpu.VMEM((rows, SSD_INNER), F32),
                        pltpu.VMEM((rows, SSD_INNER), F32),
                        pltpu.VMEM((rows, SSD_INNER), F32),
                        pltpu.VMEM((rows, LANES), F32),
                        pltpu.VMEM((SUBLANES, rows), F32),
                        pltpu.VMEM((SSD_STATE, SSD_INNER), F32)],
        compiler_params=_cparams(32, 2),
        name="ssd",
    )(u, w_in, w_in, w_in, *params, *consts)


def _merge_kernel(h_ref, xn_ref, wgate_ref, y0_ref, y1_ref, y2_ref, w0_ref, w1_ref, w2_ref,
                  wout_ref, o_ref):
    xn = xn_ref[...]
    d = h_ref.shape[1]
    merged = jnp.zeros(h_ref.shape, F32)
    for i, (y_ref, w_ref) in enumerate(((y0_ref, w0_ref), (y1_ref, w1_ref), (y2_ref, w2_ref))):
        gate = jax.nn.sigmoid(_dot(xn, wgate_ref[:, i * d:(i + 1) * d]))
        merged = merged + gate * _dot(y_ref[...], w_ref[...])
    o_ref[...] = h_ref[...] + _dot(merged.astype(BF16), wout_ref[...])


def _merge(h, xn, layer, w_in, ys, wbr, wout):
    n, d = h.shape
    rows = min(MERGE_ROWS, n)
    row_spec = pl.BlockSpec((rows, d), lambda i: (i, 0))
    y_specs = [pl.BlockSpec((rows, y.shape[1]), lambda i: (i, 0)) for y in ys]
    return pl.pallas_call(
        _merge_kernel,
        grid=(n // rows,),
        in_specs=[row_spec, row_spec, _layer_spec(w_in, layer, 1, 3 * d, W_IN_COL["gates"])] + y_specs
        + [_layer_spec(w, layer, 1) for w in wbr] + [_layer_spec(wout, layer, 1)],
        out_specs=row_spec,
        out_shape=jax.ShapeDtypeStruct((n, d), F32),
        compiler_params=_cparams(40, 1),
        name="merge",
    )(h, xn, w_in, *ys, *wbr, wout)


def _rows(v):
    return v.astype(F32)[:, None, :]


def _pad_last(w, width):
    return jnp.pad(w, [(0, 0)] * (w.ndim - 1) + [(0, width - w.shape[-1])])


def _pack_w_in(w_in):
    o = _IN_OFFS
    return jnp.concatenate([_pad_last(w_in[..., o[s0]:o[s1]], width)
                            for _, width, s0, s1 in _W_IN_PACK], axis=-1).astype(BF16)


def kernel(x, p, ffn1_norm, ffn1_gate, ffn1_up, ffn1_down, mix_norm, w_in, s5_lam_re, s5_lam_im, s5_log_step, s5_b_re, s5_b_im, s5_c_re, s5_c_im, s5_d, s5_glu, gla_gate_w2, gla_gate_b2, gla_norm, ssd_conv_w, ssd_conv_b, ssd_dt_bias, ssd_a_log, ssd_d, ssd_norm, w_br_s5, w_br_gla, w_br_ssd, w_out, ffn2_norm, ffn2_gate, ffn2_up, ffn2_down, ple_norm, ple_gate, ple_proj, final_norm):
    bsz, seq, d = x.shape
    depth = p.shape[0]
    n = bsz * seq
    bf = lambda w: w.astype(BF16)

    ffn1 = (_rows(ffn1_norm), bf(_pad_last(ffn1_gate, D_FF_PAD)), bf(_pad_last(ffn1_up, D_FF_PAD)),
            bf(jnp.pad(ffn1_down, ((0, 0), (0, D_FF_PAD - D_FF), (0, 0)))))
    ffn2 = (_rows(ffn2_norm), bf(_pad_last(ffn2_gate, D_FF_PAD)), bf(_pad_last(ffn2_up, D_FF_PAD)),
            bf(jnp.pad(ffn2_down, ((0, 0), (0, D_FF_PAD - D_FF), (0, 0)))))
    w_in_p = _pack_w_in(w_in)
    bblk, cblk, tab = jax.vmap(_s5_params)(s5_lam_re, s5_lam_im, s5_log_step, s5_b_re, s5_b_im,
                                           s5_c_re, s5_c_im)
    o = _IN_OFFS
    rep = SSD_HEADDIM
    gla_params = (bf(jnp.pad(gla_gate_w2, ((0, 0), (0, LANES - GLA_RANK), (0, 0)))),
                  _rows(gla_gate_b2), _rows(gla_norm))
    ssd_params = (bf(jnp.swapaxes(w_in[..., o[8]:o[9]], 1, 2)),
                  ssd_conv_w.astype(F32), _rows(ssd_conv_b),
                  _rows(_pad_last(ssd_dt_bias, LANES)), ssd_dt_bias.astype(F32)[:, :, None],
                  _rows(_pad_last(ssd_a_log, LANES)), ssd_a_log.astype(F32)[:, :, None],
                  _rows(jnp.repeat(ssd_a_log, rep, axis=-1)), _rows(jnp.repeat(ssd_d, rep, axis=-1)),
                  _rows(ssd_norm))
    branch_w = (bf(w_br_s5), bf(w_br_gla), bf(w_br_ssd))
    w_out_b = bf(w_out)
    mixn = _rows(mix_norm)
    s5_dskip = _rows(s5_d)
    s5_glu_b = bf(s5_glu)
    ple = (_rows(ple_norm), bf(ple_gate))
    p_rows = p.reshape(depth, n, p.shape[-1])
    ple_proj_b = bf(ple_proj)
    fn = final_norm.astype(F32).reshape(1, -1)

    h = x.reshape(n, d)
    for i in range(depth):
        h, u = _ffn_norm(h, i, *ffn1, mixn)
        y_s5 = _s5(u, bsz, seq, i, w_in_p, bblk, cblk, tab, s5_dskip, s5_glu_b)
        y_gla = _gla(u, bsz, seq, i, w_in_p, *gla_params)
        y_ssd = _ssd(u, bsz, seq, i, w_in_p, *ssd_params)
        h = _merge(h, u, i, w_in_p, (y_s5, y_gla, y_ssd), branch_w, w_out_b)
        h = _ffn_ple(h, i, *ffn2, *ple, p_rows, ple_proj_b, fn, i == depth - 1)
    return h.reshape(bsz, seq, d)
```

```python
import functools

import numpy as np
import jax
import jax.numpy as jnp
from jax import lax
from jax.experimental import pallas as pl
from jax.experimental.pallas import tpu as pltpu

F32 = jnp.float32
BF16 = jnp.bfloat16

EPS = 1e-6
D_MODEL = 1024
D_FF = 2752
PLE_DIM = 256

S5_WIDTH = 512
S5_GROUP = 16
S5_GROUPS = 32
S5_STATE = 64
S5_NSTATE = S5_GROUPS * S5_STATE

GLA_HEADS = 4
GLA_DK = 64
GLA_DV = 128
GLA_KEY = 256
GLA_VAL = 512
GLA_RANK = 16
GLA_GATE_NORM = 16.0

SSD_HEADS = 8
SSD_HEADDIM = 64
SSD_INNER = 512
SSD_GROUPS = 2
SSD_STATE = 128
SSD_CONV = 4
SSD_CONV_DIM = 1024

_IN_SIZES = (S5_WIDTH, GLA_KEY, GLA_KEY, GLA_VAL, GLA_VAL, GLA_RANK,
             SSD_INNER, SSD_CONV_DIM, SSD_HEADS, D_MODEL, D_MODEL, D_MODEL)
_IN_OFFS = tuple(int(v) for v in np.concatenate([[0], np.cumsum(_IN_SIZES)]))

_W_IN_PACK = (("gates", 3 * D_MODEL, 9, 12), ("xbc", SSD_CONV_DIM, 7, 8), ("s5", S5_WIDTH, 0, 1),
              ("v", GLA_VAL, 3, 4), ("go", GLA_VAL, 4, 5), ("z", SSD_INNER, 6, 7),
              ("q", GLA_KEY, 1, 2), ("k", GLA_KEY, 2, 3), ("lr", 128, 5, 6), ("dt", 128, 8, 9))
W_IN_COL = {}
_off = 0
for _name, _width, _s0, _s1 in _W_IN_PACK:
    assert _off % _width == 0
    W_IN_COL[_name] = _off // _width
    _off += _width
W_IN_PACKED = _off

LANES = 128
SUBLANES = 8
VMEM_BYTES_V7X = 64 * 1024 * 1024

FFN_ROWS = 512
FFN_PIECES = ((0, 11 * LANES), (11 * LANES, D_FF))
S5_ROWS = 512
GLA_ROWS = 256
GLA_CHUNK = 64
SSD_ROWS = 256
SSD_CHUNK = 128
MERGE_ROWS = 512


def _cparams(vmem_mib, ndims):
    return pltpu.CompilerParams(
        dimension_semantics=("arbitrary",) * ndims,
        vmem_limit_bytes=vmem_mib * 1024 * 1024)


def _const_spec(arr, ngrid):
    zeros = (0,) * arr.ndim
    if ngrid == 1:
        imap = lambda i: zeros
    else:
        imap = lambda b, c: zeros
    return pl.BlockSpec(arr.shape, imap, pipeline_mode=pl.Buffered(1))


def _layer_spec(arr, layer, ngrid, width=None, col=0):
    shape = (None,) + arr.shape[1:]
    idx = (layer,) + (0,) * (arr.ndim - 1)
    if width is not None:
        shape = shape[:-1] + (width,)
        idx = idx[:-1] + (col,)
    if ngrid == 1:
        imap = lambda i: idx
    else:
        imap = lambda b, c: idx
    return pl.BlockSpec(shape, imap, pipeline_mode=pl.Buffered(1))


def _rms(x, w):
    ms = jnp.mean(x * x, axis=-1, keepdims=True)
    return x * lax.rsqrt(ms + EPS) * w


def _silu(x):
    return x * jax.nn.sigmoid(x)


def _softplus(x):
    return jnp.maximum(x, 0.0) + jnp.log1p(jnp.exp(-jnp.abs(x)))


def _log_sigmoid(x):
    return jnp.minimum(x, 0.0) - jnp.log1p(jnp.exp(-jnp.abs(x)))


def _dot(a, b):
    return jnp.dot(a, b, preferred_element_type=F32)


def _dot_nt(a, b):
    return lax.dot_general(a, b, (((1,), (1,)), ((), ())), preferred_element_type=F32)


def _dot_tn(a, b):
    return lax.dot_general(a, b, (((0,), (0,)), ((), ())), preferred_element_type=F32)


def _split_bf16(x):
    hi = x.astype(BF16)
    lo = (x - hi.astype(F32)).astype(BF16)
    return hi, lo


def _ffn_core(x, nw_ref, wg_ref, wu_ref, wd_ref):
    xn = _rms(x, nw_ref[...]).astype(BF16)
    acc = jnp.zeros_like(x)
    for lo, hi in FFN_PIECES:
        g = _dot(xn, wg_ref[:, lo:hi])
        u = _dot(xn, wu_ref[:, lo:hi])
        hid = (_silu(g) * u).astype(BF16)
        acc = acc + _dot(hid, wd_ref[lo:hi, :])
    return x + 0.5 * acc


def _ffn_norm_kernel(h_ref, nw_ref, wg_ref, wu_ref, wd_ref, nw2_ref, o_ref, u_ref):
    y = _ffn_core(h_ref[...], nw_ref, wg_ref, wu_ref, wd_ref)
    o_ref[...] = y
    u_ref[...] = _rms(y, nw2_ref[...]).astype(u_ref.dtype)


def _ffn_norm(h, layer, nw, wg, wu, wd, nw2):
    n, d = h.shape
    rows = min(FFN_ROWS, n)
    row_spec = pl.BlockSpec((rows, d), lambda i: (i, 0))
    ls = lambda a: _layer_spec(a, layer, 1)
    return pl.pallas_call(
        _ffn_norm_kernel,
        grid=(n // rows,),
        in_specs=[row_spec, ls(nw), ls(wg), ls(wu), ls(wd), ls(nw2)],
        out_specs=(row_spec, row_spec),
        out_shape=(jax.ShapeDtypeStruct((n, d), F32), jax.ShapeDtypeStruct((n, d), BF16)),
        compiler_params=_cparams(54, 1),
        name="ffn_norm",
    )(h, nw, wg, wu, wd, nw2)


def _s5_kernel(xn_ref, win_ref, bblk_ref, cblk_ref, tab_ref, d_ref, wglu_ref, o_ref,
               x_s, e1_s, e2_s, st_s):
    rows = xn_ref.shape[0]
    n1 = rows // SUBLANES
    n2 = n1 // SUBLANES

    @pl.when(pl.program_id(1) == 0)
    def _():
        st_s[...] = jnp.zeros_like(st_s)

    u = _s5_project(xn_ref[...], win_ref, bblk_ref, x_s)
    for k in range(n1):
        _s5_scan_block(x_s, k, tab_ref, 0, e1_s)
    for k in range(n2):
        _s5_scan_block(e1_s, k, tab_ref, 1, e2_s)
    _s5_scan_block(e2_s, 0, tab_ref, 2, None)
    _s5_fix_block(e2_s, 0, tab_ref, 2, st_s, 0)
    _s5_fix_block(e1_s, 0, tab_ref, 1, st_s, 0)
    for k in range(1, n2):
        _s5_fix_block(e1_s, k, tab_ref, 1, e2_s, k - 1)
    _s5_fix_block(x_s, 0, tab_ref, 0, st_s, 0)
    for k in range(1, n1):
        _s5_fix_block(x_s, k, tab_ref, 0, e1_s, k - 1)
    st_s[...] = jnp.broadcast_to(e2_s[SUBLANES - 1:SUBLANES, :], st_s.shape)
    o_ref[...] = _s5_output(u, cblk_ref, d_ref, wglu_ref, x_s).astype(o_ref.dtype)


def _s5(u, bsz, seq, layer, w_in, bblk, cblk, tab, dskip, wglu):
    n, d = u.shape
    rows = S5_ROWS
    assert rows == SUBLANES ** 3 and seq % rows == 0
    nc = seq // rows
    ls = lambda a: _layer_spec(a, layer, 2)
    return pl.pallas_call(
        _s5_kernel,
        grid=(bsz, nc),
        in_specs=[pl.BlockSpec((rows, d), lambda b, c: (b * nc + c, 0)),
                  _layer_spec(w_in, layer, 2, S5_WIDTH, W_IN_COL["s5"]),
                  ls(bblk), ls(cblk), ls(tab), ls(dskip), ls(wglu)],
        out_specs=pl.BlockSpec((rows, S5_WIDTH), lambda b, c: (b * nc + c, 0)),
        out_shape=jax.ShapeDtypeStruct((n, S5_WIDTH), BF16),
        scratch_shapes=[pltpu.VMEM((rows, 2 * S5_NSTATE), F32),
                        pltpu.VMEM((rows // SUBLANES, 2 * S5_NSTATE), F32),
                        pltpu.VMEM((SUBLANES, 2 * S5_NSTATE), F32),
                        pltpu.VMEM((SUBLANES, 2 * S5_NSTATE), F32)],
        compiler_params=_cparams(48, 2),
        name="s5",
    )(u, w_in, bblk, cblk, tab, dskip, wglu)


def _ffn_ple_kernel(h_ref, nw_ref, wg_ref, wu_ref, wd_ref, pn_ref, wpg_ref, p_ref, wpp_ref, fn_ref,
                    o_ref, *, final):
    y = _ffn_core(h_ref[...], nw_ref, wg_ref, wu_ref, wd_ref)
    gate = jax.nn.sigmoid(_dot(_rms(y, pn_ref[...]).astype(BF16), wpg_ref[...]))
    y = y + gate * _dot(p_ref[...].astype(BF16), wpp_ref[...])
    if final:
        y = _rms(y, fn_ref[...])
    o_ref[...] = y


def _ffn_ple(h, layer, nw, wg, wu, wd, pn, wpg, p, wpp, fn, final):
    n, d = h.shape
    rows = min(FFN_ROWS, n)
    row_spec = pl.BlockSpec((rows, d), lambda i: (i, 0))
    ls = lambda a: _layer_spec(a, layer, 1)
    return pl.pallas_call(
        functools.partial(_ffn_ple_kernel, final=final),
        grid=(n // rows,),
        in_specs=[row_spec, ls(nw), ls(wg), ls(wu), ls(wd), ls(pn), ls(wpg),
                  pl.BlockSpec((None, rows, p.shape[2]), lambda i: (layer, i, 0)), ls(wpp),
                  _const_spec(fn, 1)],
        out_specs=row_spec,
        out_shape=jax.ShapeDtypeStruct((n, d), F32),
        compiler_params=_cparams(56, 1),
        name="ffn_ple",
    )(h, nw, wg, wu, wd, pn, wpg, p, wpp, fn)


def _block_row0(k):
    return k * SUBLANES if isinstance(k, int) else pl.multiple_of(k * SUBLANES, SUBLANES)


def _s5_scan_block(r_ref, k, tab_ref, lev, e_ref):
    p = S5_NSTATE
    r0 = _block_row0(k)
    xr = r_ref[pl.ds(r0, SUBLANES), 0:p]
    xi = r_ref[pl.ds(r0, SUBLANES), p:2 * p]
    for j, d in enumerate((1, 2, 4)):
        pr = tab_ref[lev, j, :, 0:p]
        pi = tab_ref[lev, j, :, p:2 * p]
        sr = pltpu.roll(xr, d, 0)
        si = pltpu.roll(xi, d, 0)
        xr, xi = xr + pr * sr - pi * si, xi + pr * si + pi * sr
    r_ref[pl.ds(r0, SUBLANES), 0:p] = xr
    r_ref[pl.ds(r0, SUBLANES), p:2 * p] = xi
    if e_ref is not None:
        e_ref[pl.ds(k, 1), 0:p] = xr[SUBLANES - 1:SUBLANES, :]
        e_ref[pl.ds(k, 1), p:2 * p] = xi[SUBLANES - 1:SUBLANES, :]


def _s5_fix_block(r_ref, k, tab_ref, lev, src_ref, row):
    p = S5_NSTATE
    pr = src_ref[pl.ds(row, 1), 0:p]
    pi = src_ref[pl.ds(row, 1), p:2 * p]
    r0 = _block_row0(k)
    cr = tab_ref[lev, 3, :, 0:p]
    ci = tab_ref[lev, 3, :, p:2 * p]
    xr = r_ref[pl.ds(r0, SUBLANES), 0:p]
    xi = r_ref[pl.ds(r0, SUBLANES), p:2 * p]
    r_ref[pl.ds(r0, SUBLANES), 0:p] = xr + cr * pr - ci * pi
    r_ref[pl.ds(r0, SUBLANES), p:2 * p] = xi + cr * pi + ci * pr


def _s5_project(xn, win_ref, bblk_ref, x_ref):
    u = _dot(xn, win_ref[...])
    ub = u.astype(BF16)
    hw = S5_WIDTH // 2
    hp = S5_NSTATE // 2
    for j in range(2):
        xj = _dot(ub[:, j * hw:(j + 1) * hw], bblk_ref[j])
        x_ref[:, j * hp:(j + 1) * hp] = xj[:, 0:hp]
        x_ref[:, S5_NSTATE + j * hp:S5_NSTATE + (j + 1) * hp] = xj[:, hp:2 * hp]
    return u


def _s5_output(u, cblk_ref, d_ref, wglu_ref, x_ref):
    hp = S5_NSTATE // 2
    ys = []
    for j in range(2):
        hr = x_ref[:, j * hp:(j + 1) * hp].astype(BF16)
        hi = x_ref[:, S5_NSTATE + j * hp:S5_NSTATE + (j + 1) * hp].astype(BF16)
        ys.append(_dot(hr, cblk_ref[j, 0:hp, :]) + _dot(hi, cblk_ref[j, hp:2 * hp, :]))
    y = jnp.concatenate(ys, axis=1) + d_ref[...] * u
    g = jax.nn.gelu(y, approximate=True)
    z = _dot(g.astype(BF16), wglu_ref[...])
    return g * jax.nn.sigmoid(z)


def _s5_params(lam_re, lam_im, log_step, b_re, b_im, c_re, c_im):
    lr = jnp.minimum(lam_re.astype(F32), -1e-4)
    li = lam_im.astype(F32)
    step = jnp.exp(log_step.astype(F32))[:, None]
    mag = jnp.exp(lr * step)
    ar = mag * jnp.cos(li * step)
    ai = mag * jnp.sin(li * step)
    den = lr * lr + li * li
    nr = ar - 1.0
    fr = (nr * lr + ai * li) / den
    fi = (ai * lr - nr * li) / den
    br = b_re.astype(F32)
    bi = b_im.astype(F32)
    bbr = fr[..., None] * br - fi[..., None] * bi
    bbi = fr[..., None] * bi + fi[..., None] * br
    gh = S5_GROUPS // 2
    eye = jnp.eye(gh, dtype=BF16)

    def blockdiag(m):
        g, r, c = m.shape
        return (eye[:, None, :, None] * m.astype(BF16)[:, :, None, :]).reshape(g * r, g * c)

    halves = [slice(j * gh, (j + 1) * gh) for j in range(2)]
    bt_r, bt_i = bbr.transpose(0, 2, 1), bbi.transpose(0, 2, 1)
    ct_r, ct_i = c_re.astype(F32).transpose(0, 2, 1), -c_im.astype(F32).transpose(0, 2, 1)
    bblk = jnp.stack([jnp.concatenate([blockdiag(bt_r[h]), blockdiag(bt_i[h])], axis=1) for h in halves])
    cblk = jnp.stack([jnp.concatenate([blockdiag(ct_r[h]), blockdiag(ct_i[h])], axis=0) for h in halves])

    def cmul(x, y):
        return (x[0] * y[0] - x[1] * y[1], x[0] * y[1] + x[1] * y[0])

    a1 = (ar.reshape(-1), ai.reshape(-1))
    levels = []
    base = a1
    rowid = jnp.arange(SUBLANES)[:, None]
    for _ in range(3):
        p2 = cmul(base, base)
        p4 = cmul(p2, p2)
        carry = [base]
        for _i in range(SUBLANES - 1):
            carry.append(cmul(carry[-1], base))
        p8 = carry[-1]
        tabs = []
        for dshift, pw in ((1, base), (2, p2), (4, p4)):
            msk = (rowid >= dshift).astype(F32)
            tabs.append(jnp.concatenate([msk * pw[0][None, :], msk * pw[1][None, :]], axis=1))
        tabs.append(jnp.concatenate([jnp.stack([c[0] for c in carry]),
                                     jnp.stack([c[1] for c in carry])], axis=1))
        levels.append(jnp.stack(tabs))
        base = p8
    tab = jnp.stack(levels)
    return bblk, cblk, tab


def _gla_consts():
    n = GLA_CHUNK
    g = np.zeros((8, n, n), np.float32)
    valid = np.zeros((7, n, n), np.float32)
    rows = np.arange(n)
    for lev in range(6):
        s = n >> lev
        half = s // 2
        start = rows - rows % s
        bd = start + half - 1
        lower = (rows % s) < half
        for r in range(n):
            if lower[r]:
                g[lev, r, r + 1:bd[r] + 1] = 1.0
            else:
                g[lev, r, bd[r] + 1:r + 1] = 1.0
        same = start[:, None] == start[None, :]
        valid[lev] = same & (~lower)[:, None] & lower[None, :]
    valid[6] = np.eye(n)
    g[6] = np.tril(np.ones((n, n)))
    g[7] = rows[None, :] > rows[:, None]
    g = g.reshape(8 * n, n)
    g2 = np.concatenate([g, g], axis=1)
    valid_t = np.tile(valid, (1, 1, GLA_HEADS))
    heads = np.arange(GLA_HEADS)
    eye_h = heads[:, None] == heads[None, :]
    kmask = np.kron(eye_h, np.ones((n, GLA_DK)))
    vmask = np.kron(eye_h, np.ones((n, GLA_DV)))
    smask = np.kron(eye_h, np.ones((GLA_DV, GLA_DK)))
    return g2, valid_t, kmask, vmask, smask


def _gla_kernel(xn_ref, wq_ref, wk_ref, wv_ref, wgo_ref, wlr_ref, w2_ref, b2_ref, gn_ref,
                g2_ref, valid_ref, kmask_ref, vmask_ref, smask_ref, o_ref,
                q_s, k_s, v_s, go_s, la_s, st_s):
    rows = xn_ref.shape[0]
    n = GLA_CHUNK

    @pl.when(pl.program_id(1) == 0)
    def _():
        st_s[...] = jnp.zeros_like(st_s)

    xn = xn_ref[...]
    q_s[...] = _dot(xn, wq_ref[...]) * (GLA_DK ** -0.5)
    k_s[...] = _dot(xn, wk_ref[...])
    v_s[...] = _dot(xn, wv_ref[...]).astype(BF16)
    go_s[...] = _dot(xn, wgo_ref[...])
    lr = _dot(xn, wlr_ref[...])
    xg = _dot(lr.astype(BF16), w2_ref[...]) + b2_ref[...]
    la_s[...] = _log_sigmoid(xg) * (1.0 / GLA_GATE_NORM)

    chunks = range(rows // n)
    rsl = [pl.ds(ci * n, n) for ci in chunks]
    es = []
    for ci in chunks:
        la_hi, la_lo = _split_bf16(la_s[rsl[ci], :])
        es.append(jnp.exp(_dot(g2_ref[...], jnp.concatenate([la_hi, la_lo], axis=0))))

    atts = [jnp.zeros((n, GLA_HEADS * n), F32) for _ in chunks]
    for lev in range(7):
        for ci in chunks:
            q = q_s[rsl[ci], :]
            k = k_s[rsl[ci], :]
            if lev < 6:
                el = es[ci][lev * n:(lev + 1) * n]
                q = q * el
                k = k * el
            kblk = jnp.concatenate([k.astype(BF16)] * GLA_HEADS, axis=0) * kmask_ref[...]
            atts[ci] = atts[ci] + valid_ref[lev] * _dot_nt(q.astype(BF16), kblk)

    o_intra = []
    kv = []
    for ci in chunks:
        v = v_s[rsl[ci], :]
        vblk = jnp.concatenate([v] * GLA_HEADS, axis=0) * vmask_ref[...]
        o_intra.append(_dot(atts[ci].astype(BF16), vblk))
        khat = (k_s[rsl[ci], :] * es[ci][7 * n:8 * n]).astype(BF16)
        kv.append(smask_ref[...] * _dot_tn(v, khat))

    st = st_s[...]
    for ci in chunks:
        e_cum = es[ci][6 * n:7 * n]
        o = o_intra[ci] + _dot_nt((q_s[rsl[ci], :] * e_cum).astype(BF16), st.astype(BF16))
        st = st * e_cum[n - 1:n, :] + kv[ci]
        for h in range(GLA_HEADS):
            sl = slice(h * GLA_DV, (h + 1) * GLA_DV)
            oh = _rms(o[:, sl], gn_ref[...])
            o_ref[rsl[ci], sl] = (oh * _silu(go_s[rsl[ci], sl])).astype(o_ref.dtype)
    st_s[...] = st


def _gla(u, bsz, seq, layer, w_in, w2, b2, gn):
    n, d = u.shape
    rows = GLA_ROWS
    assert seq % rows == 0 and rows % GLA_CHUNK == 0
    nc = seq // rows
    g2, valid_t, kmask, vmask, smask = _gla_consts()
    consts = (jnp.asarray(g2, BF16), jnp.asarray(valid_t, F32), jnp.asarray(kmask, BF16),
              jnp.asarray(vmask, BF16), jnp.asarray(smask, F32))
    wspec = lambda name, width: _layer_spec(w_in, layer, 2, width, W_IN_COL[name])
    ls = lambda a: _layer_spec(a, layer, 2)
    return pl.pallas_call(
        _gla_kernel,
        grid=(bsz, nc),
        in_specs=[pl.BlockSpec((rows, d), lambda b, c: (b * nc + c, 0)),
                  wspec("q", GLA_KEY), wspec("k", GLA_KEY), wspec("v", GLA_VAL), wspec("go", GLA_VAL),
                  wspec("lr", LANES), ls(w2), ls(b2), ls(gn)]
        + [_const_spec(a, 2) for a in consts],
        out_specs=pl.BlockSpec((rows, GLA_VAL), lambda b, c: (b * nc + c, 0)),
        out_shape=jax.ShapeDtypeStruct((n, GLA_VAL), BF16),
        scratch_shapes=[pltpu.VMEM((rows, GLA_KEY), F32),
                        pltpu.VMEM((rows, GLA_KEY), F32),
                        pltpu.VMEM((rows, GLA_VAL), BF16),
                        pltpu.VMEM((rows, GLA_VAL), F32),
                        pltpu.VMEM((rows, GLA_KEY), F32),
                        pltpu.VMEM((GLA_HEADS * GLA_DV, GLA_KEY), F32)],
        compiler_params=_cparams(32, 2),
        name="gla",
    )(u, w_in, w_in, w_in, w_in, w_in, w2, b2, gn, *consts)


def _ssd_kernel(xn_ref, wz_ref, wx_ref, wdc_ref, wdr_ref,
                cw_ref, cb_ref, dbc_ref, dbr_ref, alc_ref, alr_ref, ale_ref,
                de_ref, gn_ref, tril_ref, triu_ref, hmask_ref, expand_ref, o_ref,
                buf_s, xbc_s, z_s, dte_s, lae_s, lac_s, lar_s, st_s):
    rows = xn_ref.shape[0]
    n = SSD_CHUNK
    tail = SUBLANES
    gw = SSD_INNER // SSD_GROUPS
    nslab = SSD_CONV_DIM // LANES

    @pl.when(pl.program_id(1) == 0)
    def _():
        st_s[...] = jnp.zeros_like(st_s)
        buf_s[:, 0:tail, :] = jnp.zeros((nslab, tail, LANES), F32)

    xn = xn_ref[...]
    z_s[...] = _dot(xn, wz_ref[...])
    xpre = _dot(xn, wx_ref[...])
    for j in range(nslab):
        buf_s[j, tail:tail + rows, :] = xpre[:, j * LANES:(j + 1) * LANES]
    for j in range(nslab):
        sl = slice(j * LANES, (j + 1) * LANES)
        conv = cb_ref[:, sl]
        for t in range(SSD_CONV):
            conv = conv + cw_ref[t:t + 1, sl] * buf_s[j, pl.ds(tail - (SSD_CONV - 1) + t, rows), :]
        xbc_s[:, sl] = _silu(conv)
    buf_s[:, 0:tail, :] = buf_s[:, rows:rows + tail, :]

    dtc = _softplus(_dot(xn, wdc_ref[...]) + dbc_ref[...])
    lac_s[...] = dtc * (-jnp.exp(alc_ref[...]))
    dtr = _softplus(_dot_nt(wdr_ref[...], xn) + dbr_ref[...])
    lar_s[...] = dtr * (-jnp.exp(alr_ref[...]))
    d_hi = dtc.astype(BF16)
    r1 = dtc - d_hi.astype(F32)
    d_mid = r1.astype(BF16)
    d_lo = (r1 - d_mid.astype(F32)).astype(BF16)
    dte = (_dot(d_hi, expand_ref[...]) + _dot(d_mid, expand_ref[...])) + _dot(d_lo, expand_ref[...])
    dte_s[...] = dte
    lae_s[...] = dte * (-jnp.exp(ale_ref[...]))

    causal = lax.broadcasted_iota(jnp.int32, (n, n), 0) >= lax.broadcasted_iota(jnp.int32, (n, n), 1)

    chunks = range(rows // n)
    rsl = [pl.ds(ci * n, n) for ci in chunks]
    groups = range(SSD_GROUPS)
    gsl = [slice(g * gw, (g + 1) * gw) for g in groups]
    tril = tril_ref[...]
    cum_c, cum_r, cum_e = [], [], []
    for ci in chunks:
        hi, lo = _split_bf16(lac_s[rsl[ci], :])
        cum_c.append(_dot(tril, hi) + _dot(tril, lo))
        hi, lo = _split_bf16(lar_s[:, rsl[ci]])
        cum_r.append(_dot(hi, triu_ref[...]) + _dot(lo, triu_ref[...]))
        hi, lo = _split_bf16(lae_s[rsl[ci], :])
        cum_e.append(_dot(tril, hi) + _dot(tril, lo))

    bms, cms, scs, kvs, xdts = [], [], [], [], []
    for ci in chunks:
        c_last = cum_e[ci][n - 1:n, :]
        xdt = xbc_s[rsl[ci], 0:SSD_INNER] * dte_s[rsl[ci], :]
        xw = (xdt * jnp.exp(c_last - cum_e[ci])).astype(BF16)
        xdts.append(xdt)
        bm = [xbc_s[rsl[ci], SSD_INNER + g * SSD_STATE:SSD_INNER + (g + 1) * SSD_STATE].astype(BF16)
              for g in groups]
        cm = [xbc_s[rsl[ci], SSD_INNER + (SSD_GROUPS + g) * SSD_STATE:
                    SSD_INNER + (SSD_GROUPS + g + 1) * SSD_STATE].astype(BF16) for g in groups]
        bms.append(bm)
        cms.append(cm)
        scs.append([_dot_nt(cm[g], bm[g]) for g in groups])
        kvs.append([_dot_tn(bm[g], xw[:, gsl[g]]) for g in groups])

    y_diag = []
    for ci in chunks:
        slabs = []
        for g in groups:
            for j in range(2):
                l0 = g * gw + j * LANES
                xp = xdts[ci][:, l0:l0 + LANES]
                yd = jnp.zeros((n, LANES), F32)
                for s in range(2):
                    hd = g * 4 + j * 2 + s
                    diff = cum_c[ci][:, hd:hd + 1] - cum_r[ci][hd:hd + 1, :]
                    dec = jnp.exp(jnp.where(causal, diff, -jnp.inf))
                    yd = yd + _dot((scs[ci][g] * dec).astype(BF16), (xp * hmask_ref[s]).astype(BF16))
                slabs.append(yd)
        y_diag.append(slabs)

    st = [st_s[:, gsl[g]] for g in groups]
    for ci in chunks:
        e_cum = jnp.exp(cum_e[ci])
        st_decay = jnp.exp(cum_e[ci][n - 1:n, :])
        xs = xbc_s[rsl[ci], 0:SSD_INNER]
        z = z_s[rsl[ci], :]
        for g in groups:
            y_off = _dot(cms[ci][g], st[g].astype(BF16)) * e_cum[:, gsl[g]]
            st[g] = st[g] * st_decay[:, gsl[g]] + kvs[ci][g]
            ys = []
            for j in range(2):
                l0 = g * gw + j * LANES
                y = (y_diag[ci][g * 2 + j] + y_off[:, j * LANES:(j + 1) * LANES]
                     + de_ref[:, l0:l0 + LANES] * xs[:, l0:l0 + LANES])
                ys.append(y * _silu(z[:, l0:l0 + LANES]))
            y = jnp.concatenate(ys, axis=1)
            o_ref[rsl[ci], gsl[g]] = _rms(y, gn_ref[:, gsl[g]]).astype(o_ref.dtype)
    for g in groups:
        st_s[:, gsl[g]] = st[g]


def _ssd(u, bsz, seq, layer, w_in, wdr, cw, cb, dbc, dbr, alc, alr, ale, de, gn):
    n, d = u.shape
    rows = SSD_ROWS
    assert seq % rows == 0 and rows % SSD_CHUNK == 0
    nc = seq // rows
    tril = np.tril(np.ones((SSD_CHUNK, SSD_CHUNK), np.float32))
    hmask = np.zeros((2, 1, LANES), np.float32)
    hmask[0, 0, :SSD_HEADDIM] = 1.0
    hmask[1, 0, SSD_HEADDIM:] = 1.0
    expand = np.zeros((LANES, SSD_INNER), np.float32)
    for hd in range(SSD_HEADS):
        expand[hd, hd * SSD_HEADDIM:(hd + 1) * SSD_HEADDIM] = 1.0
    params = (wdr, cw, cb, dbc, dbr, alc, alr, ale, de, gn)
    consts = (jnp.asarray(tril, BF16), jnp.asarray(tril.T, BF16), jnp.asarray(hmask),
              jnp.asarray(expand, BF16))
    wspec = lambda name, width: _layer_spec(w_in, layer, 2, width, W_IN_COL[name])
    return pl.pallas_call(
        _ssd_kernel,
        grid=(bsz, nc),
        in_specs=[pl.BlockSpec((rows, d), lambda b, c: (b * nc + c, 0)),
                  wspec("z", SSD_INNER), wspec("xbc", SSD_CONV_DIM), wspec("dt", LANES)]
        + [_layer_spec(a, layer, 2) for a in params] + [_const_spec(a, 2) for a in consts],
        out_specs=pl.BlockSpec((rows, SSD_INNER), lambda b, c: (b * nc + c, 0)),
        out_shape=jax.ShapeDtypeStruct((n, SSD_INNER), BF16),
        scratch_shapes=[pltpu.VMEM((SSD_CONV_DIM // LANES, rows + 2 * SUBLANES, LANES), F32),
                        pltpu.VMEM((rows, SSD_CONV_DIM), F32),
                        pltpu.VMEM((rows, SSD_INNER), F32),
                        pltpu.VMEM((rows, SSD_INNER), F32),
                        pltpu.VMEM((rows, SSD_INNER), F32),
                        pltpu.VMEM((rows, LANES), F32),
                        pltpu.VMEM((SUBLANES, rows), F32),
                        pltpu.VMEM((SSD_STATE, SSD_INNER), F32)],
        compiler_params=_cparams(32, 2),
        name="ssd",
    )(u, w_in, w_in, w_in, *params, *consts)


def _merge_kernel(h_ref, xn_ref, wgate_ref, y0_ref, y1_ref, y2_ref, w0_ref, w1_ref, w2_ref,
                  wout_ref, o_ref):
    xn = xn_ref[...]
    d = h_ref.shape[1]
    merged = jnp.zeros(h_ref.shape, F32)
    for i, (y_ref, w_ref) in enumerate(((y0_ref, w0_ref), (y1_ref, w1_ref), (y2_ref, w2_ref))):
        gate = jax.nn.sigmoid(_dot(xn, wgate_ref[:, i * d:(i + 1) * d]))
        merged = merged + gate * _dot(y_ref[...], w_ref[...])
    o_ref[...] = h_ref[...] + _dot(merged.astype(BF16), wout_ref[...])


def _merge(h, xn, layer, w_in, ys, wbr, wout):
    n, d = h.shape
    rows = min(MERGE_ROWS, n)
    row_spec = pl.BlockSpec((rows, d), lambda i: (i, 0))
    y_specs = [pl.BlockSpec((rows, y.shape[1]), lambda i: (i, 0)) for y in ys]
    return pl.pallas_call(
        _merge_kernel,
        grid=(n // rows,),
        in_specs=[row_spec, row_spec, _layer_spec(w_in, layer, 1, 3 * d, W_IN_COL["gates"])] + y_specs
        + [_layer_spec(w, layer, 1) for w in wbr] + [_layer_spec(wout, layer, 1)],
        out_specs=row_spec,
        out_shape=jax.ShapeDtypeStruct((n, d), F32),
        compiler_params=_cparams(40, 1),
        name="merge",
    )(h, xn, w_in, *ys, *wbr, wout)


def _rows(v):
    return v.astype(F32)[:, None, :]


def _pad_last(w, width):
    return jnp.pad(w, [(0, 0)] * (w.ndim - 1) + [(0, width - w.shape[-1])])


def _pack_w_in(w_in):
    o = _IN_OFFS
    return jnp.concatenate([_pad_last(w_in[..., o[s0]:o[s1]], width)
                            for _, width, s0, s1 in _W_IN_PACK], axis=-1).astype(BF16)


def kernel(x, p, ffn1_norm, ffn1_gate, ffn1_up, ffn1_down, mix_norm, w_in, s5_lam_re, s5_lam_im, s5_log_step, s5_b_re, s5_b_im, s5_c_re, s5_c_im, s5_d, s5_glu, gla_gate_w2, gla_gate_b2, gla_norm, ssd_conv_w, ssd_conv_b, ssd_dt_bias, ssd_a_log, ssd_d, ssd_norm, w_br_s5, w_br_gla, w_br_ssd, w_out, ffn2_norm, ffn2_gate, ffn2_up, ffn2_down, ple_norm, ple_gate, ple_proj, final_norm):
    bsz, seq, d = x.shape
    depth = p.shape[0]
    n = bsz * seq
    bf = lambda w: w.astype(BF16)

    ffn1 = (_rows(ffn1_norm), bf(ffn1_gate), bf(ffn1_up), bf(ffn1_down))
    ffn2 = (_rows(ffn2_norm), bf(ffn2_gate), bf(ffn2_up), bf(ffn2_down))
    w_in_p = _pack_w_in(w_in)
    bblk, cblk, tab = jax.vmap(_s5_params)(s5_lam_re, s5_lam_im, s5_log_step, s5_b_re, s5_b_im,
                                           s5_c_re, s5_c_im)
    o = _IN_OFFS
    rep = SSD_HEADDIM
    gla_params = (bf(jnp.pad(gla_gate_w2, ((0, 0), (0, LANES - GLA_RANK), (0, 0)))),
                  _rows(gla_gate_b2), _rows(gla_norm))
    ssd_params = (bf(jnp.swapaxes(w_in[..., o[8]:o[9]], 1, 2)),
                  ssd_conv_w.astype(F32), _rows(ssd_conv_b),
                  _rows(_pad_last(ssd_dt_bias, LANES)), ssd_dt_bias.astype(F32)[:, :, None],
                  _rows(_pad_last(ssd_a_log, LANES)), ssd_a_log.astype(F32)[:, :, None],
                  _rows(jnp.repeat(ssd_a_log, rep, axis=-1)), _rows(jnp.repeat(ssd_d, rep, axis=-1)),
                  _rows(ssd_norm))
    branch_w = (bf(w_br_s5), bf(w_br_gla), bf(w_br_ssd))
    w_out_b = bf(w_out)
    mixn = _rows(mix_norm)
    s5_dskip = _rows(s5_d)
    s5_glu_b = bf(s5_glu)
    ple = (_rows(ple_norm), bf(ple_gate))
    p_rows = p.reshape(depth, n, p.shape[-1])
    ple_proj_b = bf(ple_proj)
    fn = final_norm.astype(F32).reshape(1, -1)

    h = x.reshape(n, d)
    for i in range(depth):
        h, u = _ffn_norm(h, i, *ffn1, mixn)
        y_s5 = _s5(u, bsz, seq, i, w_in_p, bblk, cblk, tab, s5_dskip, s5_glu_b)
        y_gla = _gla(u, bsz, seq, i, w_in_p, *gla_params)
        y_ssd = _ssd(u, bsz, seq, i, w_in_p, *ssd_params)
        h = _merge(h, u, i, w_in_p, (y_s5, y_gla, y_ssd), branch_w, w_out_b)
        h = _ffn_ple(h, i, *ffn2, *ple, p_rows, ple_proj_b, fn, i == depth - 1)
    return h.reshape(bsz, seq, d)
```

```python
import functools

import numpy as np
import jax
import jax.numpy as jnp
from jax import lax
from jax.experimental import pallas as pl
from jax.experimental.pallas import tpu as pltpu

F32 = jnp.float32
BF16 = jnp.bfloat16

EPS = 1e-6
D_MODEL = 1024
D_FF = 2752
PLE_DIM = 256

S5_WIDTH = 512
S5_GROUP = 16
S5_GROUPS = 32
S5_STATE = 64
S5_NSTATE = S5_GROUPS * S5_STATE

GLA_HEADS = 4
GLA_DK = 64
GLA_DV = 128
GLA_KEY = 256
GLA_VAL = 512
GLA_RANK = 16
GLA_GATE_NORM = 16.0

SSD_HEADS = 8
SSD_HEADDIM = 64
SSD_INNER = 512
SSD_GROUPS = 2
SSD_STATE = 128
SSD_CONV = 4
SSD_CONV_DIM = 1024

_IN_SIZES = (S5_WIDTH, GLA_KEY, GLA_KEY, GLA_VAL, GLA_VAL, GLA_RANK,
             SSD_INNER, SSD_CONV_DIM, SSD_HEADS, D_MODEL, D_MODEL, D_MODEL)
_IN_OFFS = tuple(int(v) for v in np.concatenate([[0], np.cumsum(_IN_SIZES)]))

_W_IN_PACK = (("gates", 3 * D_MODEL, 9, 12), ("xbc", SSD_CONV_DIM, 7, 8), ("s5", S5_WIDTH, 0, 1),
              ("v", GLA_VAL, 3, 4), ("go", GLA_VAL, 4, 5), ("z", SSD_INNER, 6, 7),
              ("q", GLA_KEY, 1, 2), ("k", GLA_KEY, 2, 3), ("lr", 128, 5, 6), ("dt", 128, 8, 9))
W_IN_COL = {}
_off = 0
for _name, _width, _s0, _s1 in _W_IN_PACK:
    assert _off % _width == 0
    W_IN_COL[_name] = _off // _width
    _off += _width
W_IN_PACKED = _off

LANES = 128
SUBLANES = 8
VMEM_BYTES_V7X = 64 * 1024 * 1024

FFN_ROWS = 512
FFN_PIECES = ((0, 11 * LANES), (11 * LANES, D_FF))
S5_ROWS = 512
FFN_CHUNK = 256
GLA_ROWS = 512
GLA_CHUNK = 64
SSD_ROWS = 512
SSD_CHUNK = 128
MERGE_ROWS = 512


def _cparams(vmem_mib, ndims):
    return pltpu.CompilerParams(
        dimension_semantics=("arbitrary",) * ndims,
        vmem_limit_bytes=vmem_mib * 1024 * 1024)


def _const_spec(arr, ngrid):
    zeros = (0,) * arr.ndim
    if ngrid == 1:
        imap = lambda i: zeros
    else:
        imap = lambda b, c: zeros
    return pl.BlockSpec(arr.shape, imap, pipeline_mode=pl.Buffered(1))


def _layer_spec(arr, layer, ngrid, width=None, col=0):
    shape = (None,) + arr.shape[1:]
    idx = (layer,) + (0,) * (arr.ndim - 1)
    if width is not None:
        shape = shape[:-1] + (width,)
        idx = idx[:-1] + (col,)
    if ngrid == 1:
        imap = lambda i: idx
    else:
        imap = lambda b, c: idx
    return pl.BlockSpec(shape, imap, pipeline_mode=pl.Buffered(1))


def _rms(x, w):
    ms = jnp.mean(x * x, axis=-1, keepdims=True)
    return x * lax.rsqrt(ms + EPS) * w


def _silu(x):
    return x * jax.nn.sigmoid(x)


def _softplus(x):
    return jnp.maximum(x, 0.0) + jnp.log1p(jnp.exp(-jnp.abs(x)))


def _log_sigmoid(x):
    return jnp.minimum(x, 0.0) - jnp.log1p(jnp.exp(-jnp.abs(x)))


def _dot(a, b):
    return jnp.dot(a, b, preferred_element_type=F32)


def _dot_nt(a, b):
    return lax.dot_general(a, b, (((1,), (1,)), ((), ())), preferred_element_type=F32)


def _dot_tn(a, b):
    return lax.dot_general(a, b, (((0,), (0,)), ((), ())), preferred_element_type=F32)


def _split_bf16(x):
    hi = x.astype(BF16)
    lo = (x - hi.astype(F32)).astype(BF16)
    return hi, lo


def _ffn_core(x, nw_ref, wg_ref, wu_ref, wd_ref):
    xn = _rms(x, nw_ref[...]).astype(BF16)
    acc = jnp.zeros_like(x)
    for lo, hi in FFN_PIECES:
        g = _dot(xn, wg_ref[:, lo:hi])
        u = _dot(xn, wu_ref[:, lo:hi])
        hid = (_silu(g) * u).astype(BF16)
        acc = acc + _dot(hid, wd_ref[lo:hi, :])
    return x + 0.5 * acc


def _interleave(main, side):
    done = 0
    for i, step in enumerate(main):
        step()
        upto = (i + 1) * len(side) // len(main)
        for s in side[done:upto]:
            s()
        done = upto


def _ffn_s5_kernel(h_ref, nw_ref, wg_ref, wu_ref, wd_ref, nw2_ref,
                   win_ref, bblk_ref, cblk_ref, tab_ref, d_ref, wglu_ref,
                   o_ref, u_ref, y_ref,
                   ukeep_s, x_s, e1_s, e2_s, st_s, *, tiles_per_seq):
    i = pl.program_id(0)

    @pl.when(i == 0)
    def _():
        ukeep_s[...] = jnp.zeros_like(ukeep_s)

    @pl.when(jnp.logical_or(i == 0, lax.rem(i + tiles_per_seq - 1, tiles_per_seq) == 0))
    def _():
        st_s[...] = jnp.zeros_like(st_s)

    rows = h_ref.shape[0]
    n1 = rows // SUBLANES
    n2 = n1 // SUBLANES
    u5 = _s5_project(ukeep_s[...], win_ref, bblk_ref, x_s)

    P = functools.partial
    scan = [P(_s5_scan_block, x_s, k, tab_ref, 0, e1_s) for k in range(n1)]
    scan += [P(_s5_scan_block, e1_s, k, tab_ref, 1, e2_s) for k in range(n2)]
    scan += [P(_s5_scan_block, e2_s, 0, tab_ref, 2, None),
             P(_s5_fix_block, e2_s, 0, tab_ref, 2, st_s, 0),
             P(_s5_fix_block, e1_s, 0, tab_ref, 1, st_s, 0)]
    scan += [P(_s5_fix_block, e1_s, k, tab_ref, 1, e2_s, k - 1) for k in range(1, n2)]
    scan += [P(_s5_fix_block, x_s, 0, tab_ref, 0, st_s, 0)]
    scan += [P(_s5_fix_block, x_s, k, tab_ref, 0, e1_s, k - 1) for k in range(1, n1)]

    x = h_ref[...]
    xn = _rms(x, nw_ref[...]).astype(BF16)
    acc = [jnp.zeros_like(x)]

    def ffn_chunk(c):
        sl = slice(c * FFN_CHUNK, min((c + 1) * FFN_CHUNK, D_FF))
        hid = (_silu(_dot(xn, wg_ref[:, sl])) * _dot(xn, wu_ref[:, sl])).astype(BF16)
        acc[0] = acc[0] + _dot(hid, wd_ref[sl, :])

    nchunks = -(-D_FF // FFN_CHUNK)
    _interleave([P(ffn_chunk, c) for c in range(nchunks)], scan)
    st_s[...] = jnp.broadcast_to(e2_s[SUBLANES - 1:SUBLANES, :], st_s.shape)

    y = x + 0.5 * acc[0]
    o_ref[...] = y
    u = _rms(y, nw2_ref[...]).astype(BF16)
    u_ref[...] = u
    ukeep_s[...] = u
    y_ref[...] = _s5_output(u5, cblk_ref, d_ref, wglu_ref, x_s).astype(y_ref.dtype)


def _ffn_s5(h, seq, layer, nw, wg, wu, wd, nw2, w_in, bblk, cblk, tab, dskip, wglu):
    n, d = h.shape
    rows = S5_ROWS
    assert rows == SUBLANES ** 3 and seq % rows == 0 and n % seq == 0
    nt = n // rows
    cur = lambda i: (jnp.minimum(i, nt - 1), 0)
    prev = lambda i: (jnp.maximum(i - 1, 0), 0)
    ls = lambda a: _layer_spec(a, layer, 1)
    return pl.pallas_call(
        functools.partial(_ffn_s5_kernel, tiles_per_seq=seq // rows),
        grid=(nt + 1,),
        in_specs=[pl.BlockSpec((rows, d), cur), ls(nw), ls(wg), ls(wu), ls(wd), ls(nw2),
                  _layer_spec(w_in, layer, 1, S5_WIDTH, W_IN_COL["s5"]),
                  ls(bblk), ls(cblk), ls(tab), ls(dskip), ls(wglu)],
        out_specs=(pl.BlockSpec((rows, d), cur), pl.BlockSpec((rows, d), cur),
                   pl.BlockSpec((rows, S5_WIDTH), prev)),
        out_shape=(jax.ShapeDtypeStruct((n, d), F32), jax.ShapeDtypeStruct((n, d), BF16),
                   jax.ShapeDtypeStruct((n, S5_WIDTH), BF16)),
        scratch_shapes=[pltpu.VMEM((rows, d), BF16),
                        pltpu.VMEM((rows, 2 * S5_NSTATE), F32),
                        pltpu.VMEM((rows // SUBLANES, 2 * S5_NSTATE), F32),
                        pltpu.VMEM((SUBLANES, 2 * S5_NSTATE), F32),
                        pltpu.VMEM((SUBLANES, 2 * S5_NSTATE), F32)],
        compiler_params=_cparams(60, 1),
        name="ffn_s5",
    )(h, nw, wg, wu, wd, nw2, w_in, bblk, cblk, tab, dskip, wglu)


def _ffn_ple_kernel(h_ref, nw_ref, wg_ref, wu_ref, wd_ref, pn_ref, wpg_ref, p_ref, wpp_ref, fn_ref,
                    o_ref, *, final):
    y = _ffn_core(h_ref[...], nw_ref, wg_ref, wu_ref, wd_ref)
    gate = jax.nn.sigmoid(_dot(_rms(y, pn_ref[...]).astype(BF16), wpg_ref[...]))
    y = y + gate * _dot(p_ref[...].astype(BF16), wpp_ref[...])
    if final:
        y = _rms(y, fn_ref[...])
    o_ref[...] = y


def _ffn_ple(h, layer, nw, wg, wu, wd, pn, wpg, p, wpp, fn, final):
    n, d = h.shape
    rows = min(FFN_ROWS, n)
    row_spec = pl.BlockSpec((rows, d), lambda i: (i, 0))
    ls = lambda a: _layer_spec(a, layer, 1)
    return pl.pallas_call(
        functools.partial(_ffn_ple_kernel, final=final),
        grid=(n // rows,),
        in_specs=[row_spec, ls(nw), ls(wg), ls(wu), ls(wd), ls(pn), ls(wpg),
                  pl.BlockSpec((None, rows, p.shape[2]), lambda i: (layer, i, 0)), ls(wpp),
                  _const_spec(fn, 1)],
        out_specs=row_spec,
        out_shape=jax.ShapeDtypeStruct((n, d), F32),
        compiler_params=_cparams(56, 1),
        name="ffn_ple",
    )(h, nw, wg, wu, wd, pn, wpg, p, wpp, fn)


def _block_row0(k):
    return k * SUBLANES if isinstance(k, int) else pl.multiple_of(k * SUBLANES, SUBLANES)


def _s5_scan_block(r_ref, k, tab_ref, lev, e_ref):
    p = S5_NSTATE
    r0 = _block_row0(k)
    xr = r_ref[pl.ds(r0, SUBLANES), 0:p]
    xi = r_ref[pl.ds(r0, SUBLANES), p:2 * p]
    for j, d in enumerate((1, 2, 4)):
        pr = tab_ref[lev, j, :, 0:p]
        pi = tab_ref[lev, j, :, p:2 * p]
        sr = pltpu.roll(xr, d, 0)
        si = pltpu.roll(xi, d, 0)
        xr, xi = xr + pr * sr - pi * si, xi + pr * si + pi * sr
    r_ref[pl.ds(r0, SUBLANES), 0:p] = xr
    r_ref[pl.ds(r0, SUBLANES), p:2 * p] = xi
    if e_ref is not None:
        e_ref[pl.ds(k, 1), 0:p] = xr[SUBLANES - 1:SUBLANES, :]
        e_ref[pl.ds(k, 1), p:2 * p] = xi[SUBLANES - 1:SUBLANES, :]


def _s5_fix_block(r_ref, k, tab_ref, lev, src_ref, row):
    p = S5_NSTATE
    pr = src_ref[pl.ds(row, 1), 0:p]
    pi = src_ref[pl.ds(row, 1), p:2 * p]
    r0 = _block_row0(k)
    cr = tab_ref[lev, 3, :, 0:p]
    ci = tab_ref[lev, 3, :, p:2 * p]
    xr = r_ref[pl.ds(r0, SUBLANES), 0:p]
    xi = r_ref[pl.ds(r0, SUBLANES), p:2 * p]
    r_ref[pl.ds(r0, SUBLANES), 0:p] = xr + cr * pr - ci * pi
    r_ref[pl.ds(r0, SUBLANES), p:2 * p] = xi + cr * pi + ci * pr


def _s5_project(xn, win_ref, bblk_ref, x_ref):
    u = _dot(xn, win_ref[...])
    ub = u.astype(BF16)
    hw = S5_WIDTH // 2
    hp = S5_NSTATE // 2
    for j in range(2):
        xj = _dot(ub[:, j * hw:(j + 1) * hw], bblk_ref[j])
        x_ref[:, j * hp:(j + 1) * hp] = xj[:, 0:hp]
        x_ref[:, S5_NSTATE + j * hp:S5_NSTATE + (j + 1) * hp] = xj[:, hp:2 * hp]
    return u


def _s5_output(u, cblk_ref, d_ref, wglu_ref, x_ref):
    hp = S5_NSTATE // 2
    ys = []
    for j in range(2):
        hr = x_ref[:, j * hp:(j + 1) * hp].astype(BF16)
        hi = x_ref[:, S5_NSTATE + j * hp:S5_NSTATE + (j + 1) * hp].astype(BF16)
        ys.append(_dot(hr, cblk_ref[j, 0:hp, :]) + _dot(hi, cblk_ref[j, hp:2 * hp, :]))
    y = jnp.concatenate(ys, axis=1) + d_ref[...] * u
    g = jax.nn.gelu(y, approximate=True)
    z = _dot(g.astype(BF16), wglu_ref[...])
    return g * jax.nn.sigmoid(z)


def _s5_params(lam_re, lam_im, log_step, b_re, b_im, c_re, c_im):
    lr = jnp.minimum(lam_re.astype(F32), -1e-4)
    li = lam_im.astype(F32)
    step = jnp.exp(log_step.astype(F32))[:, None]
    mag = jnp.exp(lr * step)
    ar = mag * jnp.cos(li * step)
    ai = mag * jnp.sin(li * step)
    den = lr * lr + li * li
    nr = ar - 1.0
    fr = (nr * lr + ai * li) / den
    fi = (ai * lr - nr * li) / den
    br = b_re.astype(F32)
    bi = b_im.astype(F32)
    bbr = fr[..., None] * br - fi[..., None] * bi
    bbi = fr[..., None] * bi + fi[..., None] * br
    gh = S5_GROUPS // 2
    eye = jnp.eye(gh, dtype=BF16)

    def blockdiag(m):
        g, r, c = m.shape
        return (eye[:, None, :, None] * m.astype(BF16)[:, :, None, :]).reshape(g * r, g * c)

    halves = [slice(j * gh, (j + 1) * gh) for j in range(2)]
    bt_r, bt_i = bbr.transpose(0, 2, 1), bbi.transpose(0, 2, 1)
    ct_r, ct_i = c_re.astype(F32).transpose(0, 2, 1), -c_im.astype(F32).transpose(0, 2, 1)
    bblk = jnp.stack([jnp.concatenate([blockdiag(bt_r[h]), blockdiag(bt_i[h])], axis=1) for h in halves])
    cblk = jnp.stack([jnp.concatenate([blockdiag(ct_r[h]), blockdiag(ct_i[h])], axis=0) for h in halves])

    def cmul(x, y):
        return (x[0] * y[0] - x[1] * y[1], x[0] * y[1] + x[1] * y[0])

    a1 = (ar.reshape(-1), ai.reshape(-1))
    levels = []
    base = a1
    rowid = jnp.arange(SUBLANES)[:, None]
    for _ in range(3):
        p2 = cmul(base, base)
        p4 = cmul(p2, p2)
        carry = [base]
        for _i in range(SUBLANES - 1):
            carry.append(cmul(carry[-1], base))
        p8 = carry[-1]
        tabs = []
        for dshift, pw in ((1, base), (2, p2), (4, p4)):
            msk = (rowid >= dshift).astype(F32)
            tabs.append(jnp.concatenate([msk * pw[0][None, :], msk * pw[1][None, :]], axis=1))
        tabs.append(jnp.concatenate([jnp.stack([c[0] for c in carry]),
                                     jnp.stack([c[1] for c in carry])], axis=1))
        levels.append(jnp.stack(tabs))
        base = p8
    tab = jnp.stack(levels)
    return bblk, cblk, tab


def _gla_consts():
    n = GLA_CHUNK
    g = np.zeros((8, n, n), np.float32)
    valid = np.zeros((7, n, n), np.float32)
    rows = np.arange(n)
    for lev in range(6):
        s = n >> lev
        half = s // 2
        start = rows - rows % s
        bd = start + half - 1
        lower = (rows % s) < half
        for r in range(n):
            if lower[r]:
                g[lev, r, r + 1:bd[r] + 1] = 1.0
            else:
                g[lev, r, bd[r] + 1:r + 1] = 1.0
        same = start[:, None] == start[None, :]
        valid[lev] = same & (~lower)[:, None] & lower[None, :]
    valid[6] = np.eye(n)
    g[6] = np.tril(np.ones((n, n)))
    g[7] = rows[None, :] > rows[:, None]
    g = g.reshape(8 * n, n)
    g2 = np.concatenate([g, g], axis=1)
    valid_t = np.tile(valid, (1, 1, GLA_HEADS))
    heads = np.arange(GLA_HEADS)
    eye_h = heads[:, None] == heads[None, :]
    kmask = np.kron(eye_h, np.ones((n, GLA_DK)))
    vmask = np.kron(eye_h, np.ones((n, GLA_DV)))
    smask = np.kron(eye_h, np.ones((GLA_DV, GLA_DK)))
    return g2, valid_t, kmask, vmask, smask


def _gla_kernel(xn_ref, wq_ref, wk_ref, wv_ref, wgo_ref, wlr_ref, w2_ref, b2_ref, gn_ref,
                g2_ref, valid_ref, kmask_ref, vmask_ref, smask_ref, o_ref,
                q_s, k_s, v_s, go_s, la_s, st_s):
    rows = xn_ref.shape[0]
    n = GLA_CHUNK

    @pl.when(pl.program_id(1) == 0)
    def _():
        st_s[...] = jnp.zeros_like(st_s)

    xn = xn_ref[...]
    q_s[...] = _dot(xn, wq_ref[...]) * (GLA_DK ** -0.5)
    k_s[...] = _dot(xn, wk_ref[...])
    v_s[...] = _dot(xn, wv_ref[...]).astype(BF16)
    go_s[...] = _dot(xn, wgo_ref[...])
    lr = _dot(xn, wlr_ref[...])
    xg = _dot(lr.astype(BF16), w2_ref[...]) + b2_ref[...]
    la_s[...] = _log_sigmoid(xg) * (1.0 / GLA_GATE_NORM)

    chunks = range(rows // n)
    rsl = [pl.ds(ci * n, n) for ci in chunks]
    es = []
    for ci in chunks:
        la_hi, la_lo = _split_bf16(la_s[rsl[ci], :])
        es.append(jnp.exp(_dot(g2_ref[...], jnp.concatenate([la_hi, la_lo], axis=0))))

    atts = [jnp.zeros((n, GLA_HEADS * n), F32) for _ in chunks]
    for lev in range(7):
        for ci in chunks:
            q = q_s[rsl[ci], :]
            k = k_s[rsl[ci], :]
            if lev < 6:
                el = es[ci][lev * n:(lev + 1) * n]
                q = q * el
                k = k * el
            kblk = jnp.concatenate([k.astype(BF16)] * GLA_HEADS, axis=0) * kmask_ref[...]
            atts[ci] = atts[ci] + valid_ref[lev] * _dot_nt(q.astype(BF16), kblk)

    o_intra = []
    kv = []
    for ci in chunks:
        v = v_s[rsl[ci], :]
        vblk = jnp.concatenate([v] * GLA_HEADS, axis=0) * vmask_ref[...]
        o_intra.append(_dot(atts[ci].astype(BF16), vblk))
        khat = (k_s[rsl[ci], :] * es[ci][7 * n:8 * n]).astype(BF16)
        kv.append(smask_ref[...] * _dot_tn(v, khat))

    st = st_s[...]
    for ci in chunks:
        e_cum = es[ci][6 * n:7 * n]
        o = o_intra[ci] + _dot_nt((q_s[rsl[ci], :] * e_cum).astype(BF16), st.astype(BF16))
        st = st * e_cum[n - 1:n, :] + kv[ci]
        for h in range(GLA_HEADS):
            sl = slice(h * GLA_DV, (h + 1) * GLA_DV)
            oh = _rms(o[:, sl], gn_ref[...])
            o_ref[rsl[ci], sl] = (oh * _silu(go_s[rsl[ci], sl])).astype(o_ref.dtype)
    st_s[...] = st


def _gla(u, bsz, seq, layer, w_in, w2, b2, gn):
    n, d = u.shape
    rows = GLA_ROWS
    assert seq % rows == 0 and rows % GLA_CHUNK == 0
    nc = seq // rows
    g2, valid_t, kmask, vmask, smask = _gla_consts()
    consts = (jnp.asarray(g2, BF16), jnp.asarray(valid_t, F32), jnp.asarray(kmask, BF16),
              jnp.asarray(vmask, BF16), jnp.asarray(smask, F32))
    wspec = lambda name, width: _layer_spec(w_in, layer, 2, width, W_IN_COL[name])
    ls = lambda a: _layer_spec(a, layer, 2)
    return pl.pallas_call(
        _gla_kernel,
        grid=(bsz, nc),
        in_specs=[pl.BlockSpec((rows, d), lambda b, c: (b * nc + c, 0)),
                  wspec("q", GLA_KEY), wspec("k", GLA_KEY), wspec("v", GLA_VAL), wspec("go", GLA_VAL),
                  wspec("lr", LANES), ls(w2), ls(b2), ls(gn)]
        + [_const_spec(a, 2) for a in consts],
        out_specs=pl.BlockSpec((rows, GLA_VAL), lambda b, c: (b * nc + c, 0)),
        out_shape=jax.ShapeDtypeStruct((n, GLA_VAL), BF16),
        scratch_shapes=[pltpu.VMEM((rows, GLA_KEY), F32),
                        pltpu.VMEM((rows, GLA_KEY), F32),
                        pltpu.VMEM((rows, GLA_VAL), BF16),
                        pltpu.VMEM((rows, GLA_VAL), F32),
                        pltpu.VMEM((rows, GLA_KEY), F32),
                        pltpu.VMEM((GLA_HEADS * GLA_DV, GLA_KEY), F32)],
        compiler_params=_cparams(32, 2),
        name="gla",
    )(u, w_in, w_in, w_in, w_in, w_in, w2, b2, gn, *consts)


def _ssd_kernel(xn_ref, wz_ref, wx_ref, wdc_ref, wdr_ref,
                cw_ref, cb_ref, dbc_ref, dbr_ref, alc_ref, alr_ref, ale_ref,
                de_ref, gn_ref, tril_ref, triu_ref, hmask_ref, expand_ref, o_ref,
                buf_s, xbc_s, z_s, dte_s, lae_s, lac_s, lar_s, st_s):
    rows = xn_ref.shape[0]
    n = SSD_CHUNK
    tail = SUBLANES
    gw = SSD_INNER // SSD_GROUPS
    nslab = SSD_CONV_DIM // LANES

    @pl.when(pl.program_id(1) == 0)
    def _():
        st_s[...] = jnp.zeros_like(st_s)
        buf_s[:, 0:tail, :] = jnp.zeros((nslab, tail, LANES), F32)

    xn = xn_ref[...]
    z_s[...] = _dot(xn, wz_ref[...])
    xpre = _dot(xn, wx_ref[...])
    for j in range(nslab):
        buf_s[j, tail:tail + rows, :] = xpre[:, j * LANES:(j + 1) * LANES]
    for j in range(nslab):
        sl = slice(j * LANES, (j + 1) * LANES)
        conv = cb_ref[:, sl]
        for t in range(SSD_CONV):
            conv = conv + cw_ref[t:t + 1, sl] * buf_s[j, pl.ds(tail - (SSD_CONV - 1) + t, rows), :]
        xbc_s[:, sl] = _silu(conv)
    buf_s[:, 0:tail, :] = buf_s[:, rows:rows + tail, :]

    dtc = _softplus(_dot(xn, wdc_ref[...]) + dbc_ref[...])
    lac_s[...] = dtc * (-jnp.exp(alc_ref[...]))
    dtr = _softplus(_dot_nt(wdr_ref[...], xn) + dbr_ref[...])
    lar_s[...] = dtr * (-jnp.exp(alr_ref[...]))
    d_hi = dtc.astype(BF16)
    r1 = dtc - d_hi.astype(F32)
    d_mid = r1.astype(BF16)
    d_lo = (r1 - d_mid.astype(F32)).astype(BF16)
    dte = (_dot(d_hi, expand_ref[...]) + _dot(d_mid, expand_ref[...])) + _dot(d_lo, expand_ref[...])
    dte_s[...] = dte
    lae_s[...] = dte * (-jnp.exp(ale_ref[...]))

    causal = lax.broadcasted_iota(jnp.int32, (n, n), 0) >= lax.broadcasted_iota(jnp.int32, (n, n), 1)

    chunks = range(rows // n)
    rsl = [pl.ds(ci * n, n) for ci in chunks]
    groups = range(SSD_GROUPS)
    gsl = [slice(g * gw, (g + 1) * gw) for g in groups]
    tril = tril_ref[...]
    cum_c, cum_r, cum_e = [], [], []
    for ci in chunks:
        hi, lo = _split_bf16(lac_s[rsl[ci], :])
        cum_c.append(_dot(tril, hi) + _dot(tril, lo))
        hi, lo = _split_bf16(lar_s[:, rsl[ci]])
        cum_r.append(_dot(hi, triu_ref[...]) + _dot(lo, triu_ref[...]))
        hi, lo = _split_bf16(lae_s[rsl[ci], :])
        cum_e.append(_dot(tril, hi) + _dot(tril, lo))

    bms, cms, scs, kvs, xdts = [], [], [], [], []
    for ci in chunks:
        c_last = cum_e[ci][n - 1:n, :]
        xdt = xbc_s[rsl[ci], 0:SSD_INNER] * dte_s[rsl[ci], :]
        xw = (xdt * jnp.exp(c_last - cum_e[ci])).astype(BF16)
        xdts.append(xdt)
        bm = [xbc_s[rsl[ci], SSD_INNER + g * SSD_STATE:SSD_INNER + (g + 1) * SSD_STATE].astype(BF16)
              for g in groups]
        cm = [xbc_s[rsl[ci], SSD_INNER + (SSD_GROUPS + g) * SSD_STATE:
                    SSD_INNER + (SSD_GROUPS + g + 1) * SSD_STATE].astype(BF16) for g in groups]
        bms.append(bm)
        cms.append(cm)
        scs.append([_dot_nt(cm[g], bm[g]) for g in groups])
        kvs.append([_dot_tn(bm[g], xw[:, gsl[g]]) for g in groups])

    y_diag = []
    for ci in chunks:
        slabs = []
        for g in groups:
            for j in range(2):
                l0 = g * gw + j * LANES
                xp = xdts[ci][:, l0:l0 + LANES]
                yd = jnp.zeros((n, LANES), F32)
                for s in range(2):
                    hd = g * 4 + j * 2 + s
                    diff = cum_c[ci][:, hd:hd + 1] - cum_r[ci][hd:hd + 1, :]
                    dec = jnp.exp(jnp.where(causal, diff, -jnp.inf))
                    yd = yd + _dot((scs[ci][g] * dec).astype(BF16), (xp * hmask_ref[s]).astype(BF16))
                slabs.append(yd)
        y_diag.append(slabs)

    st = [st_s[:, gsl[g]] for g in groups]
    for ci in chunks:
        e_cum = jnp.exp(cum_e[ci])
        st_decay = jnp.exp(cum_e[ci][n - 1:n, :])
        xs = xbc_s[rsl[ci], 0:SSD_INNER]
        z = z_s[rsl[ci], :]
        for g in groups:
            y_off = _dot(cms[ci][g], st[g].astype(BF16)) * e_cum[:, gsl[g]]
            st[g] = st[g] * st_decay[:, gsl[g]] + kvs[ci][g]
            ys = []
            for j in range(2):
                l0 = g * gw + j * LANES
                y = (y_diag[ci][g * 2 + j] + y_off[:, j * LANES:(j + 1) * LANES]
                     + de_ref[:, l0:l0 + LANES] * xs[:, l0:l0 + LANES])
                ys.append(y * _silu(z[:, l0:l0 + LANES]))
            y = jnp.concatenate(ys, axis=1)
            o_ref[rsl[ci], gsl[g]] = _rms(y, gn_ref[:, gsl[g]]).astype(o_ref.dtype)
    for g in groups:
        st_s[:, gsl[g]] = st[g]


def _ssd(u, bsz, seq, layer, w_in, wdr, cw, cb, dbc, dbr, alc, alr, ale, de, gn):
    n, d = u.shape
    rows = SSD_ROWS
    assert seq % rows == 0 and rows % SSD_CHUNK == 0
    nc = seq // rows
    tril = np.tril(np.ones((SSD_CHUNK, SSD_CHUNK), np.float32))
    hmask = np.zeros((2, 1, LANES), np.float32)
    hmask[0, 0, :SSD_HEADDIM] = 1.0
    hmask[1, 0, SSD_HEADDIM:] = 1.0
    expand = np.zeros((LANES, SSD_INNER), np.float32)
    for hd in range(SSD_HEADS):
        expand[hd, hd * SSD_HEADDIM:(hd + 1) * SSD_HEADDIM] = 1.0
    params = (wdr, cw, cb, dbc, dbr, alc, alr, ale, de, gn)
    consts = (jnp.asarray(tril, BF16), jnp.asarray(tril.T, BF16), jnp.asarray(hmask),
              jnp.asarray(expand, BF16))
    wspec = lambda name, width: _layer_spec(w_in, layer, 2, width, W_IN_COL[name])
    return pl.pallas_call(
        _ssd_kernel,
        grid=(bsz, nc),
        in_specs=[pl.BlockSpec((rows, d), lambda b, c: (b * nc + c, 0)),
                  wspec("z", SSD_INNER), wspec("xbc", SSD_CONV_DIM), wspec("dt", LANES)]
        + [_layer_spec(a, layer, 2) for a in params] + [_const_spec(a, 2) for a in consts],
        out_specs=pl.BlockSpec((rows, SSD_INNER), lambda b, c: (b * nc + c, 0)),
        out_shape=jax.ShapeDtypeStruct((n, SSD_INNER), BF16),
        scratch_shapes=[pltpu.VMEM((SSD_CONV_DIM // LANES, rows + 2 * SUBLANES, LANES), F32),
                        pltpu.VMEM((rows, SSD_CONV_DIM), F32),
                        pltpu.VMEM((rows, SSD_INNER), F32),
                        pltpu.VMEM((rows, SSD_INNER), F32),
                        pltpu.VMEM((rows, SSD_INNER), F32),
                        pltpu.VMEM((rows, LANES), F32),
                        pltpu.VMEM((SUBLANES, rows), F32),
                        pltpu.VMEM((SSD_STATE, SSD_INNER), F32)],
        compiler_params=_cparams(32, 2),
        name="ssd",
    )(u, w_in, w_in, w_in, *params, *consts)


def _merge_kernel(h_ref, xn_ref, wgate_ref, y0_ref, y1_ref, y2_ref, w0_ref, w1_ref, w2_ref,
                  wout_ref, o_ref):
    xn = xn_ref[...]
    d = h_ref.shape[1]
    merged = jnp.zeros(h_ref.shape, F32)
    for i, (y_ref, w_ref) in enumerate(((y0_ref, w0_ref), (y1_ref, w1_ref), (y2_ref, w2_ref))):
        gate = jax.nn.sigmoid(_dot(xn, wgate_ref[:, i * d:(i + 1) * d]))
        merged = merged + gate * _dot(y_ref[...], w_ref[...])
    o_ref[...] = h_ref[...] + _dot(merged.astype(BF16), wout_ref[...])


def _merge(h, xn, layer, w_in, ys, wbr, wout):
    n, d = h.shape
    rows = min(MERGE_ROWS, n)
    row_spec = pl.BlockSpec((rows, d), lambda i: (i, 0))
    y_specs = [pl.BlockSpec((rows, y.shape[1]), lambda i: (i, 0)) for y in ys]
    return pl.pallas_call(
        _merge_kernel,
        grid=(n // rows,),
        in_specs=[row_spec, row_spec, _layer_spec(w_in, layer, 1, 3 * d, W_IN_COL["gates"])] + y_specs
        + [_layer_spec(w, layer, 1) for w in wbr] + [_layer_spec(wout, layer, 1)],
        out_specs=row_spec,
        out_shape=jax.ShapeDtypeStruct((n, d), F32),
        compiler_params=_cparams(40, 1),
        name="merge",
    )(h, xn, w_in, *ys, *wbr, wout)


def _rows(v):
    return v.astype(F32)[:, None, :]


def _pad_last(w, width):
    return jnp.pad(w, [(0, 0)] * (w.ndim - 1) + [(0, width - w.shape[-1])])


def _pack_w_in(w_in):
    o = _IN_OFFS
    return jnp.concatenate([_pad_last(w_in[..., o[s0]:o[s1]], width)
                            for _, width, s0, s1 in _W_IN_PACK], axis=-1).astype(BF16)


def kernel(x, p, ffn1_norm, ffn1_gate, ffn1_up, ffn1_down, mix_norm, w_in, s5_lam_re, s5_lam_im, s5_log_step, s5_b_re, s5_b_im, s5_c_re, s5_c_im, s5_d, s5_glu, gla_gate_w2, gla_gate_b2, gla_norm, ssd_conv_w, ssd_conv_b, ssd_dt_bias, ssd_a_log, ssd_d, ssd_norm, w_br_s5, w_br_gla, w_br_ssd, w_out, ffn2_norm, ffn2_gate, ffn2_up, ffn2_down, ple_norm, ple_gate, ple_proj, final_norm):
    bsz, seq, d = x.shape
    depth = p.shape[0]
    n = bsz * seq
    bf = lambda w: w.astype(BF16)

    ffn1 = (_rows(ffn1_norm), bf(ffn1_gate), bf(ffn1_up), bf(ffn1_down))
    ffn2 = (_rows(ffn2_norm), bf(ffn2_gate), bf(ffn2_up), bf(ffn2_down))
    w_in_p = _pack_w_in(w_in)
    bblk, cblk, tab = jax.vmap(_s5_params)(s5_lam_re, s5_lam_im, s5_log_step, s5_b_re, s5_b_im,
                                           s5_c_re, s5_c_im)
    o = _IN_OFFS
    rep = SSD_HEADDIM
    gla_params = (bf(jnp.pad(gla_gate_w2, ((0, 0), (0, LANES - GLA_RANK), (0, 0)))),
                  _rows(gla_gate_b2), _rows(gla_norm))
    ssd_params = (bf(jnp.swapaxes(w_in[..., o[8]:o[9]], 1, 2)),
                  ssd_conv_w.astype(F32), _rows(ssd_conv_b),
                  _rows(_pad_last(ssd_dt_bias, LANES)), ssd_dt_bias.astype(F32)[:, :, None],
                  _rows(_pad_last(ssd_a_log, LANES)), ssd_a_log.astype(F32)[:, :, None],
                  _rows(jnp.repeat(ssd_a_log, rep, axis=-1)), _rows(jnp.repeat(ssd_d, rep, axis=-1)),
                  _rows(ssd_norm))
    branch_w = (bf(w_br_s5), bf(w_br_gla), bf(w_br_ssd))
    w_out_b = bf(w_out)
    mixn = _rows(mix_norm)
    s5_dskip = _rows(s5_d)
    s5_glu_b = bf(s5_glu)
    ple = (_rows(ple_norm), bf(ple_gate))
    p_rows = p.reshape(depth, n, p.shape[-1])
    ple_proj_b = bf(ple_proj)
    fn = final_norm.astype(F32).reshape(1, -1)

    h = x.reshape(n, d)
    for i in range(depth):
        h, u, y_s5 = _ffn_s5(h, seq, i, *ffn1, mixn, w_in_p, bblk, cblk, tab, s5_dskip, s5_glu_b)
        y_gla = _gla(u, bsz, seq, i, w_in_p, *gla_params)
        y_ssd = _ssd(u, bsz, seq, i, w_in_p, *ssd_params)
        h = _merge(h, u, i, w_in_p, (y_s5, y_gla, y_ssd), branch_w, w_out_b)
        h = _ffn_ple(h, i, *ffn2, *ple, p_rows, ple_proj_b, fn, i == depth - 1)
    return h.reshape(bsz, seq, d)
```

```python
import functools

import numpy as np
import jax
import jax.numpy as jnp
from jax import lax
from jax.experimental import pallas as pl
from jax.experimental.pallas import tpu as pltpu

F32 = jnp.float32
BF16 = jnp.bfloat16

EPS = 1e-6
D_MODEL = 1024
D_FF = 2752
PLE_DIM = 256

S5_WIDTH = 512
S5_GROUP = 16
S5_GROUPS = 32
S5_STATE = 64
S5_NSTATE = S5_GROUPS * S5_STATE

GLA_HEADS = 4
GLA_DK = 64
GLA_DV = 128
GLA_KEY = 256
GLA_VAL = 512
GLA_RANK = 16
GLA_GATE_NORM = 16.0

SSD_HEADS = 8
SSD_HEADDIM = 64
SSD_INNER = 512
SSD_GROUPS = 2
SSD_STATE = 128
SSD_CONV = 4
SSD_CONV_DIM = 1024

_IN_SIZES = (S5_WIDTH, GLA_KEY, GLA_KEY, GLA_VAL, GLA_VAL, GLA_RANK,
             SSD_INNER, SSD_CONV_DIM, SSD_HEADS, D_MODEL, D_MODEL, D_MODEL)
_IN_OFFS = tuple(int(v) for v in np.concatenate([[0], np.cumsum(_IN_SIZES)]))

_W_IN_PACK = (("gates", 3 * D_MODEL, 9, 12), ("xbc", SSD_CONV_DIM, 7, 8), ("s5", S5_WIDTH, 0, 1),
              ("v", GLA_VAL, 3, 4), ("go", GLA_VAL, 4, 5), ("z", SSD_INNER, 6, 7),
              ("q", GLA_KEY, 1, 2), ("k", GLA_KEY, 2, 3), ("lr", 128, 5, 6), ("dt", 128, 8, 9))
W_IN_COL = {}
_off = 0
for _name, _width, _s0, _s1 in _W_IN_PACK:
    assert _off % _width == 0
    W_IN_COL[_name] = _off // _width
    _off += _width
W_IN_PACKED = _off

LANES = 128
SUBLANES = 8
VMEM_BYTES_V7X = 64 * 1024 * 1024

FFN_ROWS = 512
FFN_PIECES = ((0, 11 * LANES), (11 * LANES, D_FF))
S5_ROWS = 512
FFN_CHUNK = 256
GLA_ROWS = 256
GLA_CHUNK = 64
SSD_ROWS = 512
SSD_CHUNK = 128
MERGE_ROWS = 512


def _cparams(vmem_mib, ndims):
    return pltpu.CompilerParams(
        dimension_semantics=("arbitrary",) * ndims,
        vmem_limit_bytes=vmem_mib * 1024 * 1024)


def _const_spec(arr, ngrid):
    zeros = (0,) * arr.ndim
    if ngrid == 1:
        imap = lambda i: zeros
    else:
        imap = lambda b, c: zeros
    return pl.BlockSpec(arr.shape, imap, pipeline_mode=pl.Buffered(1))


def _layer_spec(arr, layer, ngrid, width=None, col=0):
    shape = (None,) + arr.shape[1:]
    idx = (layer,) + (0,) * (arr.ndim - 1)
    if width is not None:
        shape = shape[:-1] + (width,)
        idx = idx[:-1] + (col,)
    if ngrid == 1:
        imap = lambda i: idx
    else:
        imap = lambda b, c: idx
    return pl.BlockSpec(shape, imap, pipeline_mode=pl.Buffered(1))


def _rms(x, w):
    ms = jnp.mean(x * x, axis=-1, keepdims=True)
    return x * lax.rsqrt(ms + EPS) * w


def _silu(x):
    return x * jax.nn.sigmoid(x)


def _softplus(x):
    return jnp.maximum(x, 0.0) + jnp.log1p(jnp.exp(-jnp.abs(x)))


def _log_sigmoid(x):
    return jnp.minimum(x, 0.0) - jnp.log1p(jnp.exp(-jnp.abs(x)))


def _dot(a, b):
    return jnp.dot(a, b, preferred_element_type=F32)


def _dot_nt(a, b):
    return lax.dot_general(a, b, (((1,), (1,)), ((), ())), preferred_element_type=F32)


def _dot_tn(a, b):
    return lax.dot_general(a, b, (((0,), (0,)), ((), ())), preferred_element_type=F32)


def _split_bf16(x):
    hi = x.astype(BF16)
    lo = (x - hi.astype(F32)).astype(BF16)
    return hi, lo


def _ffn_core(x, nw_ref, wg_ref, wu_ref, wd_ref):
    xn = _rms(x, nw_ref[...]).astype(BF16)
    acc = jnp.zeros_like(x)
    for lo, hi in FFN_PIECES:
        g = _dot(xn, wg_ref[:, lo:hi])
        u = _dot(xn, wu_ref[:, lo:hi])
        hid = (_silu(g) * u).astype(BF16)
        acc = acc + _dot(hid, wd_ref[lo:hi, :])
    return x + 0.5 * acc


def _interleave(main, side):
    done = 0
    for i, step in enumerate(main):
        step()
        upto = (i + 1) * len(side) // len(main)
        for s in side[done:upto]:
            s()
        done = upto


def _ffn_s5_kernel(h_ref, nw_ref, wg_ref, wu_ref, wd_ref, nw2_ref,
                   win_ref, bblk_ref, cblk_ref, tab_ref, d_ref, wglu_ref,
                   o_ref, u_ref, y_ref,
                   ukeep_s, x_s, e1_s, e2_s, st_s, *, tiles_per_seq):
    i = pl.program_id(0)

    @pl.when(i == 0)
    def _():
        ukeep_s[...] = jnp.zeros_like(ukeep_s)
        for ref in (x_s, e1_s, e2_s):
            ref[:, 0:S5_PAD, :] = jnp.zeros((2 * S5_SLABS, S5_PAD, LANES), F32)

    @pl.when(jnp.logical_or(i == 0, lax.rem(i + tiles_per_seq - 1, tiles_per_seq) == 0))
    def _():
        st_s[...] = jnp.zeros_like(st_s)

    rows = h_ref.shape[0]
    n1 = rows // SUBLANES
    n2 = n1 // SUBLANES
    u5 = _s5_project(ukeep_s[...], win_ref, bblk_ref, x_s)

    P = functools.partial
    scan = [P(_s5_scan_block, x_s, k, tab_ref, 0, e1_s) for k in range(n1)]
    scan += [P(_s5_scan_block, e1_s, k, tab_ref, 1, e2_s) for k in range(n2)]
    scan += [P(_s5_scan_block, e2_s, 0, tab_ref, 2, None),
             P(_s5_fix_block, e2_s, 0, tab_ref, 2, st_s, 0),
             P(_s5_fix_block, e1_s, 0, tab_ref, 1, st_s, 0)]
    scan += [P(_s5_fix_block, e1_s, k, tab_ref, 1, e2_s, S5_PAD + k - 1) for k in range(1, n2)]
    scan += [P(_s5_fix_block, x_s, 0, tab_ref, 0, st_s, 0)]
    scan += [P(_s5_fix_block, x_s, k, tab_ref, 0, e1_s, S5_PAD + k - 1) for k in range(1, n1)]

    x = h_ref[...]
    xn = _rms(x, nw_ref[...]).astype(BF16)
    acc = [jnp.zeros_like(x)]

    def ffn_chunk(c):
        sl = slice(c * FFN_CHUNK, min((c + 1) * FFN_CHUNK, D_FF))
        hid = (_silu(_dot(xn, wg_ref[:, sl])) * _dot(xn, wu_ref[:, sl])).astype(BF16)
        acc[0] = acc[0] + _dot(hid, wd_ref[sl, :])

    nchunks = -(-D_FF // FFN_CHUNK)
    _interleave([P(ffn_chunk, c) for c in range(nchunks)], scan)
    last = S5_PAD + n2 - 1
    st_s[...] = jnp.broadcast_to(e2_s[:, last:last + 1, :], st_s.shape)

    y = x + 0.5 * acc[0]
    o_ref[...] = y
    u = _rms(y, nw2_ref[...]).astype(BF16)
    u_ref[...] = u
    ukeep_s[...] = u
    y_ref[...] = _s5_output(u5, cblk_ref, d_ref, wglu_ref, x_s).astype(y_ref.dtype)


def _ffn_s5(h, seq, layer, nw, wg, wu, wd, nw2, w_in, bblk, cblk, tab, dskip, wglu):
    n, d = h.shape
    rows = S5_ROWS
    assert rows == SUBLANES ** 3 and seq % rows == 0 and n % seq == 0
    nt = n // rows
    cur = lambda i: (jnp.minimum(i, nt - 1), 0)
    prev = lambda i: (jnp.maximum(i - 1, 0), 0)
    ls = lambda a: _layer_spec(a, layer, 1)
    return pl.pallas_call(
        functools.partial(_ffn_s5_kernel, tiles_per_seq=seq // rows),
        grid=(nt + 1,),
        in_specs=[pl.BlockSpec((rows, d), cur), ls(nw), ls(wg), ls(wu), ls(wd), ls(nw2),
                  _layer_spec(w_in, layer, 1, S5_WIDTH, W_IN_COL["s5"]),
                  ls(bblk), ls(cblk), ls(tab), ls(dskip), ls(wglu)],
        out_specs=(pl.BlockSpec((rows, d), cur), pl.BlockSpec((rows, d), cur),
                   pl.BlockSpec((rows, S5_WIDTH), prev)),
        out_shape=(jax.ShapeDtypeStruct((n, d), F32), jax.ShapeDtypeStruct((n, d), BF16),
                   jax.ShapeDtypeStruct((n, S5_WIDTH), BF16)),
        scratch_shapes=[pltpu.VMEM((rows, d), BF16),
                        pltpu.VMEM((2 * S5_SLABS, S5_PAD + rows, LANES), F32),
                        pltpu.VMEM((2 * S5_SLABS, S5_PAD + rows // SUBLANES, LANES), F32),
                        pltpu.VMEM((2 * S5_SLABS, S5_PAD + SUBLANES, LANES), F32),
                        pltpu.VMEM((2 * S5_SLABS, SUBLANES, LANES), F32)],
        compiler_params=_cparams(60, 1),
        name="ffn_s5",
    )(h, nw, wg, wu, wd, nw2, w_in, bblk, cblk, tab, dskip, wglu)


def _ffn_ple_kernel(h_ref, nw_ref, wg_ref, wu_ref, wd_ref, pn_ref, wpg_ref, p_ref, wpp_ref, fn_ref,
                    o_ref, *, final):
    y = _ffn_core(h_ref[...], nw_ref, wg_ref, wu_ref, wd_ref)
    gate = jax.nn.sigmoid(_dot(_rms(y, pn_ref[...]).astype(BF16), wpg_ref[...]))
    y = y + gate * _dot(p_ref[...].astype(BF16), wpp_ref[...])
    if final:
        y = _rms(y, fn_ref[...])
    o_ref[...] = y


def _ffn_ple(h, layer, nw, wg, wu, wd, pn, wpg, p, wpp, fn, final):
    n, d = h.shape
    rows = min(FFN_ROWS, n)
    row_spec = pl.BlockSpec((rows, d), lambda i: (i, 0))
    ls = lambda a: _layer_spec(a, layer, 1)
    return pl.pallas_call(
        functools.partial(_ffn_ple_kernel, final=final),
        grid=(n // rows,),
        in_specs=[row_spec, ls(nw), ls(wg), ls(wu), ls(wd), ls(pn), ls(wpg),
                  pl.BlockSpec((None, rows, p.shape[2]), lambda i: (layer, i, 0)), ls(wpp),
                  _const_spec(fn, 1)],
        out_specs=row_spec,
        out_shape=jax.ShapeDtypeStruct((n, d), F32),
        compiler_params=_cparams(56, 1),
        name="ffn_ple",
    )(h, nw, wg, wu, wd, pn, wpg, p, wpp, fn)


S5_SLABS = S5_NSTATE // LANES
S5_PAD = SUBLANES


def _s5_scan_block(r_ref, k, tab_ref, lev, e_ref):
    r0 = S5_PAD + k * SUBLANES
    for s in range(S5_SLABS):
        lanes = slice(s * LANES, (s + 1) * LANES)
        im = S5_NSTATE + s * LANES
        xr = r_ref[s, pl.ds(r0, SUBLANES), :]
        xi = r_ref[S5_SLABS + s, pl.ds(r0, SUBLANES), :]
        for j, d in enumerate((1, 2, 4)):
            if j > 0:
                r_ref[s, pl.ds(r0, SUBLANES), :] = xr
                r_ref[S5_SLABS + s, pl.ds(r0, SUBLANES), :] = xi
            pr = tab_ref[lev, j, :, lanes]
            pi = tab_ref[lev, j, :, im:im + LANES]
            sr = r_ref[s, pl.ds(r0 - d, SUBLANES), :]
            si = r_ref[S5_SLABS + s, pl.ds(r0 - d, SUBLANES), :]
            xr, xi = xr + pr * sr - pi * si, xi + pr * si + pi * sr
        r_ref[s, pl.ds(r0, SUBLANES), :] = xr
        r_ref[S5_SLABS + s, pl.ds(r0, SUBLANES), :] = xi
        if e_ref is not None:
            e_ref[s, pl.ds(S5_PAD + k, 1), :] = xr[SUBLANES - 1:SUBLANES, :]
            e_ref[S5_SLABS + s, pl.ds(S5_PAD + k, 1), :] = xi[SUBLANES - 1:SUBLANES, :]


def _s5_fix_block(r_ref, k, tab_ref, lev, src_ref, row):
    r0 = S5_PAD + k * SUBLANES
    for s in range(S5_SLABS):
        im = S5_NSTATE + s * LANES
        pr = src_ref[s, pl.ds(row, 1), :]
        pi = src_ref[S5_SLABS + s, pl.ds(row, 1), :]
        cr = tab_ref[lev, 3, :, s * LANES:(s + 1) * LANES]
        ci = tab_ref[lev, 3, :, im:im + LANES]
        xr = r_ref[s, pl.ds(r0, SUBLANES), :]
        xi = r_ref[S5_SLABS + s, pl.ds(r0, SUBLANES), :]
        r_ref[s, pl.ds(r0, SUBLANES), :] = xr + cr * pr - ci * pi
        r_ref[S5_SLABS + s, pl.ds(r0, SUBLANES), :] = xi + cr * pi + ci * pr


def _s5_project(xn, win_ref, bblk_ref, x_ref):
    rows = xn.shape[0]
    u = _dot(xn, win_ref[...])
    ub = u.astype(BF16)
    hw = S5_WIDTH // 2
    hs = S5_SLABS // 2
    for j in range(2):
        xj = _dot(ub[:, j * hw:(j + 1) * hw], bblk_ref[j])
        for t in range(2 * hs):
            slab = (S5_SLABS if t >= hs else 0) + j * hs + t % hs
            x_ref[slab, pl.ds(S5_PAD, rows), :] = xj[:, t * LANES:(t + 1) * LANES]
    return u


def _s5_output(u, cblk_ref, d_ref, wglu_ref, x_ref):
    rows = u.shape[0]
    hp = S5_NSTATE // 2
    hs = S5_SLABS // 2
    ys = []
    for j in range(2):
        hr = jnp.concatenate([x_ref[j * hs + t, pl.ds(S5_PAD, rows), :].astype(BF16)
                              for t in range(hs)], axis=1)
        hi = jnp.concatenate([x_ref[S5_SLABS + j * hs + t, pl.ds(S5_PAD, rows), :].astype(BF16)
                              for t in range(hs)], axis=1)
        ys.append(_dot(hr, cblk_ref[j, 0:hp, :]) + _dot(hi, cblk_ref[j, hp:2 * hp, :]))
    y = jnp.concatenate(ys, axis=1) + d_ref[...] * u
    g = jax.nn.gelu(y, approximate=True)
    z = _dot(g.astype(BF16), wglu_ref[...])
    return g * jax.nn.sigmoid(z)


def _s5_params(lam_re, lam_im, log_step, b_re, b_im, c_re, c_im):
    lr = jnp.minimum(lam_re.astype(F32), -1e-4)
    li = lam_im.astype(F32)
    step = jnp.exp(log_step.astype(F32))[:, None]
    mag = jnp.exp(lr * step)
    ar = mag * jnp.cos(li * step)
    ai = mag * jnp.sin(li * step)
    den = lr * lr + li * li
    nr = ar - 1.0
    fr = (nr * lr + ai * li) / den
    fi = (ai * lr - nr * li) / den
    br = b_re.astype(F32)
    bi = b_im.astype(F32)
    bbr = fr[..., None] * br - fi[..., None] * bi
    bbi = fr[..., None] * bi + fi[..., None] * br
    gh = S5_GROUPS // 2
    eye = jnp.eye(gh, dtype=BF16)

    def blockdiag(m):
        g, r, c = m.shape
        return (eye[:, None, :, None] * m.astype(BF16)[:, :, None, :]).reshape(g * r, g * c)

    halves = [slice(j * gh, (j + 1) * gh) for j in range(2)]
    bt_r, bt_i = bbr.transpose(0, 2, 1), bbi.transpose(0, 2, 1)
    ct_r, ct_i = c_re.astype(F32).transpose(0, 2, 1), -c_im.astype(F32).transpose(0, 2, 1)
    bblk = jnp.stack([jnp.concatenate([blockdiag(bt_r[h]), blockdiag(bt_i[h])], axis=1) for h in halves])
    cblk = jnp.stack([jnp.concatenate([blockdiag(ct_r[h]), blockdiag(ct_i[h])], axis=0) for h in halves])

    def cmul(x, y):
        return (x[0] * y[0] - x[1] * y[1], x[0] * y[1] + x[1] * y[0])

    a1 = (ar.reshape(-1), ai.reshape(-1))
    levels = []
    base = a1
    rowid = jnp.arange(SUBLANES)[:, None]
    for _ in range(3):
        p2 = cmul(base, base)
        p4 = cmul(p2, p2)
        carry = [base]
        for _i in range(SUBLANES - 1):
            carry.append(cmul(carry[-1], base))
        p8 = carry[-1]
        tabs = []
        for dshift, pw in ((1, base), (2, p2), (4, p4)):
            msk = (rowid >= dshift).astype(F32)
            tabs.append(jnp.concatenate([msk * pw[0][None, :], msk * pw[1][None, :]], axis=1))
        tabs.append(jnp.concatenate([jnp.stack([c[0] for c in carry]),
                                     jnp.stack([c[1] for c in carry])], axis=1))
        levels.append(jnp.stack(tabs))
        base = p8
    tab = jnp.stack(levels)
    return bblk, cblk, tab


def _gla_consts():
    n = GLA_CHUNK
    g = np.zeros((8, n, n), np.float32)
    valid = np.zeros((7, n, n), np.float32)
    rows = np.arange(n)
    for lev in range(6):
        s = n >> lev
        half = s // 2
        start = rows - rows % s
        bd = start + half - 1
        lower = (rows % s) < half
        for r in range(n):
            if lower[r]:
                g[lev, r, r + 1:bd[r] + 1] = 1.0
            else:
                g[lev, r, bd[r] + 1:r + 1] = 1.0
        same = start[:, None] == start[None, :]
        valid[lev] = same & (~lower)[:, None] & lower[None, :]
    valid[6] = np.eye(n)
    g[6] = np.tril(np.ones((n, n)))
    g[7] = rows[None, :] > rows[:, None]
    g = g.reshape(8 * n, n)
    g2 = np.concatenate([g, g], axis=1)
    valid_t = np.tile(valid, (1, 1, GLA_HEADS))
    heads = np.arange(GLA_HEADS)
    eye_h = heads[:, None] == heads[None, :]
    kmask = np.kron(eye_h, np.ones((n, GLA_DK)))
    vmask = np.kron(eye_h, np.ones((n, GLA_DV)))
    smask = np.kron(eye_h, np.ones((GLA_DV, GLA_DK)))
    return g2, valid_t, kmask, vmask, smask


def _gla_kernel(xn_ref, wq_ref, wk_ref, wv_ref, wgo_ref, wlr_ref, w2_ref, b2_ref, gn_ref,
                g2_ref, valid_ref, kmask_ref, vmask_ref, smask_ref, o_ref,
                q_s, k_s, v_s, go_s, la_s, st_s):
    rows = xn_ref.shape[0]
    n = GLA_CHUNK

    @pl.when(pl.program_id(1) == 0)
    def _():
        st_s[...] = jnp.zeros_like(st_s)

    xn = xn_ref[...]
    q_s[...] = _dot(xn, wq_ref[...]) * (GLA_DK ** -0.5)
    k_s[...] = _dot(xn, wk_ref[...])
    v_s[...] = _dot(xn, wv_ref[...]).astype(BF16)
    go_s[...] = _dot(xn, wgo_ref[...])
    lr = _dot(xn, wlr_ref[...])
    xg = _dot(lr.astype(BF16), w2_ref[...]) + b2_ref[...]
    la_s[...] = _log_sigmoid(xg) * (1.0 / GLA_GATE_NORM)

    chunks = range(rows // n)
    rsl = [pl.ds(ci * n, n) for ci in chunks]
    es = []
    for ci in chunks:
        la_hi, la_lo = _split_bf16(la_s[rsl[ci], :])
        es.append(jnp.exp(_dot(g2_ref[...], jnp.concatenate([la_hi, la_lo], axis=0))))

    atts = [jnp.zeros((n, GLA_HEADS * n), F32) for _ in chunks]
    for lev in range(7):
        for ci in chunks:
            q = q_s[rsl[ci], :]
            k = k_s[rsl[ci], :]
            if lev < 6:
                el = es[ci][lev * n:(lev + 1) * n]
                q = q * el
                k = k * el
            kblk = jnp.concatenate([k.astype(BF16)] * GLA_HEADS, axis=0) * kmask_ref[...]
            atts[ci] = atts[ci] + valid_ref[lev] * _dot_nt(q.astype(BF16), kblk)

    o_intra = []
    kv = []
    for ci in chunks:
        v = v_s[rsl[ci], :]
        vblk = jnp.concatenate([v] * GLA_HEADS, axis=0) * vmask_ref[...]
        o_intra.append(_dot(atts[ci].astype(BF16), vblk))
        khat = (k_s[rsl[ci], :] * es[ci][7 * n:8 * n]).astype(BF16)
        kv.append(smask_ref[...] * _dot_tn(v, khat))

    st = st_s[...]
    for ci in chunks:
        e_cum = es[ci][6 * n:7 * n]
        o = o_intra[ci] + _dot_nt((q_s[rsl[ci], :] * e_cum).astype(BF16), st.astype(BF16))
        st = st * e_cum[n - 1:n, :] + kv[ci]
        for h in range(GLA_HEADS):
            sl = slice(h * GLA_DV, (h + 1) * GLA_DV)
            oh = _rms(o[:, sl], gn_ref[...])
            o_ref[rsl[ci], sl] = (oh * _silu(go_s[rsl[ci], sl])).astype(o_ref.dtype)
    st_s[...] = st


def _gla(u, bsz, seq, layer, w_in, w2, b2, gn):
    n, d = u.shape
    rows = GLA_ROWS
    assert seq % rows == 0 and rows % GLA_CHUNK == 0
    nc = seq // rows
    g2, valid_t, kmask, vmask, smask = _gla_consts()
    consts = (jnp.asarray(g2, BF16), jnp.asarray(valid_t, F32), jnp.asarray(kmask, BF16),
              jnp.asarray(vmask, BF16), jnp.asarray(smask, F32))
    wspec = lambda name, width: _layer_spec(w_in, layer, 2, width, W_IN_COL[name])
    ls = lambda a: _layer_spec(a, layer, 2)
    return pl.pallas_call(
        _gla_kernel,
        grid=(bsz, nc),
        in_specs=[pl.BlockSpec((rows, d), lambda b, c: (b * nc + c, 0)),
                  wspec("q", GLA_KEY), wspec("k", GLA_KEY), wspec("v", GLA_VAL), wspec("go", GLA_VAL),
                  wspec("lr", LANES), ls(w2), ls(b2), ls(gn)]
        + [_const_spec(a, 2) for a in consts],
        out_specs=pl.BlockSpec((rows, GLA_VAL), lambda b, c: (b * nc + c, 0)),
        out_shape=jax.ShapeDtypeStruct((n, GLA_VAL), BF16),
        scratch_shapes=[pltpu.VMEM((rows, GLA_KEY), F32),
                        pltpu.VMEM((rows, GLA_KEY), F32),
                        pltpu.VMEM((rows, GLA_VAL), BF16),
                        pltpu.VMEM((rows, GLA_VAL), F32),
                        pltpu.VMEM((rows, GLA_KEY), F32),
                        pltpu.VMEM((GLA_HEADS * GLA_DV, GLA_KEY), F32)],
        compiler_params=_cparams(32, 2),
        name="gla",
    )(u, w_in, w_in, w_in, w_in, w_in, w2, b2, gn, *consts)


def _ssd_kernel(xn_ref, wz_ref, wx_ref, wdc_ref, wdr_ref,
                cw_ref, cb_ref, dbc_ref, dbr_ref, alc_ref, alr_ref, ale_ref,
                de_ref, gn_ref, tril_ref, triu_ref, hmask_ref, expand_ref, o_ref,
                buf_s, xbc_s, z_s, dte_s, lae_s, lac_s, lar_s, st_s):
    rows = xn_ref.shape[0]
    n = SSD_CHUNK
    tail = SUBLANES
    gw = SSD_INNER // SSD_GROUPS
    nslab = SSD_CONV_DIM // LANES

    @pl.when(pl.program_id(1) == 0)
    def _():
        st_s[...] = jnp.zeros_like(st_s)
        buf_s[:, 0:tail, :] = jnp.zeros((nslab, tail, LANES), F32)

    xn = xn_ref[...]
    z_s[...] = _dot(xn, wz_ref[...])
    xpre = _dot(xn, wx_ref[...])
    for j in range(nslab):
        buf_s[j, tail:tail + rows, :] = xpre[:, j * LANES:(j + 1) * LANES]
    for j in range(nslab):
        sl = slice(j * LANES, (j + 1) * LANES)
        conv = cb_ref[:, sl]
        for t in range(SSD_CONV):
            conv = conv + cw_ref[t:t + 1, sl] * buf_s[j, pl.ds(tail - (SSD_CONV - 1) + t, rows), :]
        xbc_s[:, sl] = _silu(conv)
    buf_s[:, 0:tail, :] = buf_s[:, rows:rows + tail, :]

    dtc = _softplus(_dot(xn, wdc_ref[...]) + dbc_ref[...])
    lac_s[...] = dtc * (-jnp.exp(alc_ref[...]))
    dtr = _softplus(_dot_nt(wdr_ref[...], xn) + dbr_ref[...])
    lar_s[...] = dtr * (-jnp.exp(alr_ref[...]))
    d_hi = dtc.astype(BF16)
    r1 = dtc - d_hi.astype(F32)
    d_mid = r1.astype(BF16)
    d_lo = (r1 - d_mid.astype(F32)).astype(BF16)
    dte = (_dot(d_hi, expand_ref[...]) + _dot(d_mid, expand_ref[...])) + _dot(d_lo, expand_ref[...])
    dte_s[...] = dte
    lae_s[...] = dte * (-jnp.exp(ale_ref[...]))

    causal = lax.broadcasted_iota(jnp.int32, (n, n), 0) >= lax.broadcasted_iota(jnp.int32, (n, n), 1)

    chunks = range(rows // n)
    rsl = [pl.ds(ci * n, n) for ci in chunks]
    groups = range(SSD_GROUPS)
    gsl = [slice(g * gw, (g + 1) * gw) for g in groups]
    tril = tril_ref[...]
    cum_c, cum_r, cum_e = [], [], []
    for ci in chunks:
        hi, lo = _split_bf16(lac_s[rsl[ci], :])
        cum_c.append(_dot(tril, hi) + _dot(tril, lo))
        hi, lo = _split_bf16(lar_s[:, rsl[ci]])
        cum_r.append(_dot(hi, triu_ref[...]) + _dot(lo, triu_ref[...]))
        hi, lo = _split_bf16(lae_s[rsl[ci], :])
        cum_e.append(_dot(tril, hi) + _dot(tril, lo))

    bms, cms, scs, kvs, xdts = [], [], [], [], []
    for ci in chunks:
        c_last = cum_e[ci][n - 1:n, :]
        xdt = xbc_s[rsl[ci], 0:SSD_INNER] * dte_s[rsl[ci], :]
        xw = (xdt * jnp.exp(c_last - cum_e[ci])).astype(BF16)
        xdts.append(xdt)
        bm = [xbc_s[rsl[ci], SSD_INNER + g * SSD_STATE:SSD_INNER + (g + 1) * SSD_STATE].astype(BF16)
              for g in groups]
        cm = [xbc_s[rsl[ci], SSD_INNER + (SSD_GROUPS + g) * SSD_STATE:
                    SSD_INNER + (SSD_GROUPS + g + 1) * SSD_STATE].astype(BF16) for g in groups]
        bms.append(bm)
        cms.append(cm)
        scs.append([_dot_nt(cm[g], bm[g]) for g in groups])
        kvs.append([_dot_tn(bm[g], xw[:, gsl[g]]) for g in groups])

    y_diag = []
    for ci in chunks:
        slabs = []
        for g in groups:
            for j in range(2):
                l0 = g * gw + j * LANES
                xp = xdts[ci][:, l0:l0 + LANES]
                yd = jnp.zeros((n, LANES), F32)
                for s in range(2):
                    hd = g * 4 + j * 2 + s
                    diff = cum_c[ci][:, hd:hd + 1] - cum_r[ci][hd:hd + 1, :]
                    dec = jnp.exp(jnp.where(causal, diff, -jnp.inf))
                    yd = yd + _dot((scs[ci][g] * dec).astype(BF16), (xp * hmask_ref[s]).astype(BF16))
                slabs.append(yd)
        y_diag.append(slabs)

    st = [st_s[:, gsl[g]] for g in groups]
    for ci in chunks:
        e_cum = jnp.exp(cum_e[ci])
        st_decay = jnp.exp(cum_e[ci][n - 1:n, :])
        xs = xbc_s[rsl[ci], 0:SSD_INNER]
        z = z_s[rsl[ci], :]
        for g in groups:
            y_off = _dot(cms[ci][g], st[g].astype(BF16)) * e_cum[:, gsl[g]]
            st[g] = st[g] * st_decay[:, gsl[g]] + kvs[ci][g]
            ys = []
            for j in range(2):
                l0 = g * gw + j * LANES
                y = (y_diag[ci][g * 2 + j] + y_off[:, j * LANES:(j + 1) * LANES]
                     + de_ref[:, l0:l0 + LANES] * xs[:, l0:l0 + LANES])
                ys.append(y * _silu(z[:, l0:l0 + LANES]))
            y = jnp.concatenate(ys, axis=1)
            o_ref[rsl[ci], gsl[g]] = _rms(y, gn_ref[:, gsl[g]]).astype(o_ref.dtype)
    for g in groups:
        st_s[:, gsl[g]] = st[g]


def _ssd(u, bsz, seq, layer, w_in, wdr, cw, cb, dbc, dbr, alc, alr, ale, de, gn):
    n, d = u.shape
    rows = SSD_ROWS
    assert seq % rows == 0 and rows % SSD_CHUNK == 0
    nc = seq // rows
    tril = np.tril(np.ones((SSD_CHUNK, SSD_CHUNK), np.float32))
    hmask = np.zeros((2, 1, LANES), np.float32)
    hmask[0, 0, :SSD_HEADDIM] = 1.0
    hmask[1, 0, SSD_HEADDIM:] = 1.0
    expand = np.zeros((LANES, SSD_INNER), np.float32)
    for hd in range(SSD_HEADS):
        expand[hd, hd * SSD_HEADDIM:(hd + 1) * SSD_HEADDIM] = 1.0
    params = (wdr, cw, cb, dbc, dbr, alc, alr, ale, de, gn)
    consts = (jnp.asarray(tril, BF16), jnp.asarray(tril.T, BF16), jnp.asarray(hmask),
              jnp.asarray(expand, BF16))
    wspec = lambda name, width: _layer_spec(w_in, layer, 2, width, W_IN_COL[name])
    return pl.pallas_call(
        _ssd_kernel,
        grid=(bsz, nc),
        in_specs=[pl.BlockSpec((rows, d), lambda b, c: (b * nc + c, 0)),
                  wspec("z", SSD_INNER), wspec("xbc", SSD_CONV_DIM), wspec("dt", LANES)]
        + [_layer_spec(a, layer, 2) for a in params] + [_const_spec(a, 2) for a in consts],
        out_specs=pl.BlockSpec((rows, SSD_INNER), lambda b, c: (b * nc + c, 0)),
        out_shape=jax.ShapeDtypeStruct((n, SSD_INNER), BF16),
        scratch_shapes=[pltpu.VMEM((SSD_CONV_DIM // LANES, rows + 2 * SUBLANES, LANES), F32),
                        pltpu.VMEM((rows, SSD_CONV_DIM), F32),
                        pltpu.VMEM((rows, SSD_INNER), F32),
                        pltpu.VMEM((rows, SSD_INNER), F32),
                        pltpu.VMEM((rows, SSD_INNER), F32),
                        pltpu.VMEM((rows, LANES), F32),
                        pltpu.VMEM((SUBLANES, rows), F32),
                        pltpu.VMEM((SSD_STATE, SSD_INNER), F32)],
        compiler_params=_cparams(32, 2),
        name="ssd",
    )(u, w_in, w_in, w_in, *params, *consts)


def _merge_kernel(h_ref, xn_ref, wgate_ref, y0_ref, y1_ref, y2_ref, w0_ref, w1_ref, w2_ref,
                  wout_ref, o_ref):
    xn = xn_ref[...]
    d = h_ref.shape[1]
    merged = jnp.zeros(h_ref.shape, F32)
    for i, (y_ref, w_ref) in enumerate(((y0_ref, w0_ref), (y1_ref, w1_ref), (y2_ref, w2_ref))):
        gate = jax.nn.sigmoid(_dot(xn, wgate_ref[:, i * d:(i + 1) * d]))
        merged = merged + gate * _dot(y_ref[...], w_ref[...])
    o_ref[...] = h_ref[...] + _dot(merged.astype(BF16), wout_ref[...])


def _merge(h, xn, layer, w_in, ys, wbr, wout):
    n, d = h.shape
    rows = min(MERGE_ROWS, n)
    row_spec = pl.BlockSpec((rows, d), lambda i: (i, 0))
    y_specs = [pl.BlockSpec((rows, y.shape[1]), lambda i: (i, 0)) for y in ys]
    return pl.pallas_call(
        _merge_kernel,
        grid=(n // rows,),
        in_specs=[row_spec, row_spec, _layer_spec(w_in, layer, 1, 3 * d, W_IN_COL["gates"])] + y_specs
        + [_layer_spec(w, layer, 1) for w in wbr] + [_layer_spec(wout, layer, 1)],
        out_specs=row_spec,
        out_shape=jax.ShapeDtypeStruct((n, d), F32),
        compiler_params=_cparams(40, 1),
        name="merge",
    )(h, xn, w_in, *ys, *wbr, wout)


def _rows(v):
    return v.astype(F32)[:, None, :]


def _pad_last(w, width):
    return jnp.pad(w, [(0, 0)] * (w.ndim - 1) + [(0, width - w.shape[-1])])


def _pack_w_in(w_in):
    o = _IN_OFFS
    return jnp.concatenate([_pad_last(w_in[..., o[s0]:o[s1]], width)
                            for _, width, s0, s1 in _W_IN_PACK], axis=-1).astype(BF16)


def kernel(x, p, ffn1_norm, ffn1_gate, ffn1_up, ffn1_down, mix_norm, w_in, s5_lam_re, s5_lam_im, s5_log_step, s5_b_re, s5_b_im, s5_c_re, s5_c_im, s5_d, s5_glu, gla_gate_w2, gla_gate_b2, gla_norm, ssd_conv_w, ssd_conv_b, ssd_dt_bias, ssd_a_log, ssd_d, ssd_norm, w_br_s5, w_br_gla, w_br_ssd, w_out, ffn2_norm, ffn2_gate, ffn2_up, ffn2_down, ple_norm, ple_gate, ple_proj, final_norm):
    bsz, seq, d = x.shape
    depth = p.shape[0]
    n = bsz * seq
    bf = lambda w: w.astype(BF16)

    ffn1 = (_rows(ffn1_norm), bf(ffn1_gate), bf(ffn1_up), bf(ffn1_down))
    ffn2 = (_rows(ffn2_norm), bf(ffn2_gate), bf(ffn2_up), bf(ffn2_down))
    w_in_p = _pack_w_in(w_in)
    bblk, cblk, tab = jax.vmap(_s5_params)(s5_lam_re, s5_lam_im, s5_log_step, s5_b_re, s5_b_im,
                                           s5_c_re, s5_c_im)
    o = _IN_OFFS
    rep = SSD_HEADDIM
    gla_params = (bf(jnp.pad(gla_gate_w2, ((0, 0), (0, LANES - GLA_RANK), (0, 0)))),
                  _rows(gla_gate_b2), _rows(gla_norm))
    ssd_params = (bf(jnp.swapaxes(w_in[..., o[8]:o[9]], 1, 2)),
                  ssd_conv_w.astype(F32), _rows(ssd_conv_b),
                  _rows(_pad_last(ssd_dt_bias, LANES)), ssd_dt_bias.astype(F32)[:, :, None],
                  _rows(_pad_last(ssd_a_log, LANES)), ssd_a_log.astype(F32)[:, :, None],
                  _rows(jnp.repeat(ssd_a_log, rep, axis=-1)), _rows(jnp.repeat(ssd_d, rep, axis=-1)),
                  _rows(ssd_norm))
    branch_w = (bf(w_br_s5), bf(w_br_gla), bf(w_br_ssd))
    w_out_b = bf(w_out)
    mixn = _rows(mix_norm)
    s5_dskip = _rows(s5_d)
    s5_glu_b = bf(s5_glu)
    ple = (_rows(ple_norm), bf(ple_gate))
    p_rows = p.reshape(depth, n, p.shape[-1])
    ple_proj_b = bf(ple_proj)
    fn = final_norm.astype(F32).reshape(1, -1)

    h = x.reshape(n, d)
    for i in range(depth):
        h, u, y_s5 = _ffn_s5(h, seq, i, *ffn1, mixn, w_in_p, bblk, cblk, tab, s5_dskip, s5_glu_b)
        y_gla = _gla(u, bsz, seq, i, w_in_p, *gla_params)
        y_ssd = _ssd(u, bsz, seq, i, w_in_p, *ssd_params)
        h = _merge(h, u, i, w_in_p, (y_s5, y_gla, y_ssd), branch_w, w_out_b)
        h = _ffn_ple(h, i, *ffn2, *ple, p_rows, ple_proj_b, fn, i == depth - 1)
    return h.reshape(bsz, seq, d)
```

```python
import functools

import numpy as np
import jax
import jax.numpy as jnp
from jax import lax
from jax.experimental import pallas as pl
from jax.experimental.pallas import tpu as pltpu

F32 = jnp.float32
BF16 = jnp.bfloat16

EPS = 1e-6
D_MODEL = 1024
D_FF = 2752
PLE_DIM = 256

S5_WIDTH = 512
S5_GROUP = 16
S5_GROUPS = 32
S5_STATE = 64
S5_NSTATE = S5_GROUPS * S5_STATE

GLA_HEADS = 4
GLA_DK = 64
GLA_DV = 128
GLA_KEY = 256
GLA_VAL = 512
GLA_RANK = 16
GLA_GATE_NORM = 16.0

SSD_HEADS = 8
SSD_HEADDIM = 64
SSD_INNER = 512
SSD_GROUPS = 2
SSD_STATE = 128
SSD_CONV = 4
SSD_CONV_DIM = 1024

_IN_SIZES = (S5_WIDTH, GLA_KEY, GLA_KEY, GLA_VAL, GLA_VAL, GLA_RANK,
             SSD_INNER, SSD_CONV_DIM, SSD_HEADS, D_MODEL, D_MODEL, D_MODEL)
_IN_OFFS = tuple(int(v) for v in np.concatenate([[0], np.cumsum(_IN_SIZES)]))

_W_IN_PACK = (("gates", 3 * D_MODEL, 9, 12), ("xbc", SSD_CONV_DIM, 7, 8), ("s5", S5_WIDTH, 0, 1),
              ("v", GLA_VAL, 3, 4), ("go", GLA_VAL, 4, 5), ("z", SSD_INNER, 6, 7),
              ("q", GLA_KEY, 1, 2), ("k", GLA_KEY, 2, 3), ("lr", 128, 5, 6), ("dt", 128, 8, 9))
W_IN_COL = {}
_off = 0
for _name, _width, _s0, _s1 in _W_IN_PACK:
    assert _off % _width == 0
    W_IN_COL[_name] = _off // _width
    _off += _width
W_IN_PACKED = _off

LANES = 128
SUBLANES = 8
VMEM_BYTES_V7X = 64 * 1024 * 1024

FFN_ROWS = 512
FFN_PIECES = ((0, 11 * LANES), (11 * LANES, D_FF))
S5_ROWS = 512
FFN_CHUNK = 256
GLA_ROWS = 256
GLA_CHUNK = 64
SSD_ROWS = 512
SSD_CHUNK = 128
MERGE_ROWS = 512


def _cparams(vmem_mib, ndims):
    return pltpu.CompilerParams(
        dimension_semantics=("arbitrary",) * ndims,
        vmem_limit_bytes=vmem_mib * 1024 * 1024)


def _const_spec(arr, ngrid):
    zeros = (0,) * arr.ndim
    if ngrid == 1:
        imap = lambda i: zeros
    else:
        imap = lambda b, c: zeros
    return pl.BlockSpec(arr.shape, imap, pipeline_mode=pl.Buffered(1))


def _layer_spec(arr, layer, ngrid, width=None, col=0):
    shape = (None,) + arr.shape[1:]
    idx = (layer,) + (0,) * (arr.ndim - 1)
    if width is not None:
        shape = shape[:-1] + (width,)
        idx = idx[:-1] + (col,)
    if ngrid == 1:
        imap = lambda i: idx
    else:
        imap = lambda b, c: idx
    return pl.BlockSpec(shape, imap, pipeline_mode=pl.Buffered(1))


def _rms(x, w):
    ms = jnp.mean(x * x, axis=-1, keepdims=True)
    return x * lax.rsqrt(ms + EPS) * w


def _silu(x):
    return x * jax.nn.sigmoid(x)


def _softplus(x):
    return jnp.maximum(x, 0.0) + jnp.log1p(jnp.exp(-jnp.abs(x)))


def _log_sigmoid(x):
    return jnp.minimum(x, 0.0) - jnp.log1p(jnp.exp(-jnp.abs(x)))


def _dot(a, b):
    return jnp.dot(a, b, preferred_element_type=F32)


def _dot_nt(a, b):
    return lax.dot_general(a, b, (((1,), (1,)), ((), ())), preferred_element_type=F32)


def _dot_tn(a, b):
    return lax.dot_general(a, b, (((0,), (0,)), ((), ())), preferred_element_type=F32)


def _split_bf16(x):
    hi = x.astype(BF16)
    lo = (x - hi.astype(F32)).astype(BF16)
    return hi, lo


def _ffn_core(x, nw_ref, wg_ref, wu_ref, wd_ref):
    xn = _rms(x, nw_ref[...]).astype(BF16)
    acc = jnp.zeros_like(x)
    for lo, hi in FFN_PIECES:
        g = _dot(xn, wg_ref[:, lo:hi])
        u = _dot(xn, wu_ref[:, lo:hi])
        hid = (_silu(g) * u).astype(BF16)
        acc = acc + _dot(hid, wd_ref[lo:hi, :])
    return x + 0.5 * acc


def _interleave(main, side):
    done = 0
    for i, step in enumerate(main):
        step()
        upto = (i + 1) * len(side) // len(main)
        for s in side[done:upto]:
            s()
        done = upto


def _ffn_s5_kernel(h_ref, nw_ref, wg_ref, wu_ref, wd_ref, nw2_ref,
                   win_ref, bblk_ref, cblk_ref, tab_ref, d_ref, wglu_ref,
                   o_ref, u_ref, y_ref,
                   ukeep_s, x_s, e1_s, e2_s, st_s, *, tiles_per_seq):
    i = pl.program_id(0)

    @pl.when(i == 0)
    def _():
        ukeep_s[...] = jnp.zeros_like(ukeep_s)

    @pl.when(jnp.logical_or(i == 0, lax.rem(i + tiles_per_seq - 1, tiles_per_seq) == 0))
    def _():
        st_s[...] = jnp.zeros_like(st_s)

    rows = h_ref.shape[0]
    n1 = rows // SUBLANES
    n2 = n1 // SUBLANES
    u5 = _s5_project(ukeep_s[...], win_ref, bblk_ref, x_s)

    P = functools.partial
    scan = [P(_s5_scan_block, x_s, k, tab_ref, 0, e1_s) for k in range(n1)]
    scan += [P(_s5_scan_block, e1_s, k, tab_ref, 1, e2_s) for k in range(n2)]
    scan += [P(_s5_scan_block, e2_s, 0, tab_ref, 2, None),
             P(_s5_fix_block, e2_s, 0, tab_ref, 2, st_s, 0),
             P(_s5_fix_block, e1_s, 0, tab_ref, 1, st_s, 0)]
    scan += [P(_s5_fix_block, e1_s, k, tab_ref, 1, e2_s, k - 1) for k in range(1, n2)]
    scan += [P(_s5_fix_block, x_s, 0, tab_ref, 0, st_s, 0)]
    scan += [P(_s5_fix_block, x_s, k, tab_ref, 0, e1_s, k - 1) for k in range(1, n1)]

    x = h_ref[...]
    xn = _rms(x, nw_ref[...]).astype(BF16)
    acc = [jnp.zeros_like(x)]

    def ffn_chunk(c):
        sl = slice(c * FFN_CHUNK, min((c + 1) * FFN_CHUNK, D_FF))
        hid = (_silu(_dot(xn, wg_ref[:, sl])) * _dot(xn, wu_ref[:, sl])).astype(BF16)
        acc[0] = acc[0] + _dot(hid, wd_ref[sl, :])

    nchunks = -(-D_FF // FFN_CHUNK)
    _interleave([P(ffn_chunk, c) for c in range(nchunks)], scan)
    st_s[...] = jnp.broadcast_to(e2_s[SUBLANES - 1:SUBLANES, :], st_s.shape)

    y = x + 0.5 * acc[0]
    o_ref[...] = y
    u = _rms(y, nw2_ref[...]).astype(BF16)
    u_ref[...] = u
    ukeep_s[...] = u
    y_ref[...] = _s5_output(u5, cblk_ref, d_ref, wglu_ref, x_s).astype(y_ref.dtype)


def _ffn_s5(h, seq, layer, nw, wg, wu, wd, nw2, w_in, bblk, cblk, tab, dskip, wglu):
    n, d = h.shape
    rows = S5_ROWS
    assert rows == SUBLANES ** 3 and seq % rows == 0 and n % seq == 0
    nt = n // rows
    cur = lambda i: (jnp.minimum(i, nt - 1), 0)
    prev = lambda i: (jnp.maximum(i - 1, 0), 0)
    ls = lambda a: _layer_spec(a, layer, 1)
    return pl.pallas_call(
        functools.partial(_ffn_s5_kernel, tiles_per_seq=seq // rows),
        grid=(nt + 1,),
        in_specs=[pl.BlockSpec((rows, d), cur), ls(nw), ls(wg), ls(wu), ls(wd), ls(nw2),
                  _layer_spec(w_in, layer, 1, S5_WIDTH, W_IN_COL["s5"]),
                  ls(bblk), ls(cblk), ls(tab), ls(dskip), ls(wglu)],
        out_specs=(pl.BlockSpec((rows, d), cur), pl.BlockSpec((rows, d), cur),
                   pl.BlockSpec((rows, S5_WIDTH), prev)),
        out_shape=(jax.ShapeDtypeStruct((n, d), F32), jax.ShapeDtypeStruct((n, d), BF16),
                   jax.ShapeDtypeStruct((n, S5_WIDTH), BF16)),
        scratch_shapes=[pltpu.VMEM((rows, d), BF16),
                        pltpu.VMEM((rows, 2 * S5_NSTATE), F32),
                        pltpu.VMEM((rows // SUBLANES, 2 * S5_NSTATE), F32),
                        pltpu.VMEM((SUBLANES, 2 * S5_NSTATE), F32),
                        pltpu.VMEM((SUBLANES, 2 * S5_NSTATE), F32)],
        compiler_params=_cparams(60, 1),
        name="ffn_s5",
    )(h, nw, wg, wu, wd, nw2, w_in, bblk, cblk, tab, dskip, wglu)


def _ffn_ple_kernel(h_ref, nw_ref, wg_ref, wu_ref, wd_ref, pn_ref, wpg_ref, p_ref, wpp_ref, fn_ref,
                    o_ref, *, final):
    y = _ffn_core(h_ref[...], nw_ref, wg_ref, wu_ref, wd_ref)
    gate = jax.nn.sigmoid(_dot(_rms(y, pn_ref[...]).astype(BF16), wpg_ref[...]))
    y = y + gate * _dot(p_ref[...].astype(BF16), wpp_ref[...])
    if final:
        y = _rms(y, fn_ref[...])
    o_ref[...] = y


def _ffn_ple(h, layer, nw, wg, wu, wd, pn, wpg, p, wpp, fn, final):
    n, d = h.shape
    rows = min(FFN_ROWS, n)
    row_spec = pl.BlockSpec((rows, d), lambda i: (i, 0))
    ls = lambda a: _layer_spec(a, layer, 1)
    return pl.pallas_call(
        functools.partial(_ffn_ple_kernel, final=final),
        grid=(n // rows,),
        in_specs=[row_spec, ls(nw), ls(wg), ls(wu), ls(wd), ls(pn), ls(wpg),
                  pl.BlockSpec((None, rows, p.shape[2]), lambda i: (layer, i, 0)), ls(wpp),
                  _const_spec(fn, 1)],
        out_specs=row_spec,
        out_shape=jax.ShapeDtypeStruct((n, d), F32),
        compiler_params=_cparams(56, 1),
        name="ffn_ple",
    )(h, nw, wg, wu, wd, pn, wpg, p, wpp, fn)


def _s5_scan_block(r_ref, k, tab_ref, lev, e_ref):
    p = S5_NSTATE
    r0 = k * SUBLANES
    xr = r_ref[pl.ds(r0, SUBLANES), 0:p]
    xi = r_ref[pl.ds(r0, SUBLANES), p:2 * p]
    for j, d in enumerate((1, 2, 4)):
        pr = tab_ref[lev, j, :, 0:p]
        pi = tab_ref[lev, j, :, p:2 * p]
        sr = pltpu.roll(xr, d, 0)
        si = pltpu.roll(xi, d, 0)
        xr, xi = xr + pr * sr - pi * si, xi + pr * si + pi * sr
    r_ref[pl.ds(r0, SUBLANES), 0:p] = xr
    r_ref[pl.ds(r0, SUBLANES), p:2 * p] = xi
    if e_ref is not None:
        e_ref[pl.ds(k, 1), 0:p] = xr[SUBLANES - 1:SUBLANES, :]
        e_ref[pl.ds(k, 1), p:2 * p] = xi[SUBLANES - 1:SUBLANES, :]


def _s5_fix_block(r_ref, k, tab_ref, lev, src_ref, row):
    p = S5_NSTATE
    pr = src_ref[pl.ds(row, 1), 0:p]
    pi = src_ref[pl.ds(row, 1), p:2 * p]
    r0 = k * SUBLANES
    cr = tab_ref[lev, 3, :, 0:p]
    ci = tab_ref[lev, 3, :, p:2 * p]
    xr = r_ref[pl.ds(r0, SUBLANES), 0:p]
    xi = r_ref[pl.ds(r0, SUBLANES), p:2 * p]
    r_ref[pl.ds(r0, SUBLANES), 0:p] = xr + cr * pr - ci * pi
    r_ref[pl.ds(r0, SUBLANES), p:2 * p] = xi + cr * pi + ci * pr


def _s5_project(xn, win_ref, bblk_ref, x_ref):
    u = _dot(xn, win_ref[...])
    ub = u.astype(BF16)
    hw = S5_WIDTH // 2
    hp = S5_NSTATE // 2
    for j in range(2):
        xj = _dot(ub[:, j * hw:(j + 1) * hw], bblk_ref[j])
        x_ref[:, j * hp:(j + 1) * hp] = xj[:, 0:hp]
        x_ref[:, S5_NSTATE + j * hp:S5_NSTATE + (j + 1) * hp] = xj[:, hp:2 * hp]
    return u


def _s5_output(u, cblk_ref, d_ref, wglu_ref, x_ref):
    hp = S5_NSTATE // 2
    ys = []
    for j in range(2):
        hr = x_ref[:, j * hp:(j + 1) * hp].astype(BF16)
        hi = x_ref[:, S5_NSTATE + j * hp:S5_NSTATE + (j + 1) * hp].astype(BF16)
        ys.append(_dot(hr, cblk_ref[j, 0:hp, :]) + _dot(hi, cblk_ref[j, hp:2 * hp, :]))
    y = jnp.concatenate(ys, axis=1) + d_ref[...] * u
    g = jax.nn.gelu(y, approximate=True)
    z = _dot(g.astype(BF16), wglu_ref[...])
    return g * jax.nn.sigmoid(z)


def _s5_params(lam_re, lam_im, log_step, b_re, b_im, c_re, c_im):
    lr = jnp.minimum(lam_re.astype(F32), -1e-4)
    li = lam_im.astype(F32)
    step = jnp.exp(log_step.astype(F32))[:, None]
    mag = jnp.exp(lr * step)
    ar = mag * jnp.cos(li * step)
    ai = mag * jnp.sin(li * step)
    den = lr * lr + li * li
    nr = ar - 1.0
    fr = (nr * lr + ai * li) / den
    fi = (ai * lr - nr * li) / den
    br = b_re.astype(F32)
    bi = b_im.astype(F32)
    bbr = fr[..., None] * br - fi[..., None] * bi
    bbi = fr[..., None] * bi + fi[..., None] * br
    gh = S5_GROUPS // 2
    eye = jnp.eye(gh, dtype=BF16)

    def blockdiag(m):
        g, r, c = m.shape
        return (eye[:, None, :, None] * m.astype(BF16)[:, :, None, :]).reshape(g * r, g * c)

    halves = [slice(j * gh, (j + 1) * gh) for j in range(2)]
    bt_r, bt_i = bbr.transpose(0, 2, 1), bbi.transpose(0, 2, 1)
    ct_r, ct_i = c_re.astype(F32).transpose(0, 2, 1), -c_im.astype(F32).transpose(0, 2, 1)
    bblk = jnp.stack([jnp.concatenate([blockdiag(bt_r[h]), blockdiag(bt_i[h])], axis=1) for h in halves])
    cblk = jnp.stack([jnp.concatenate([blockdiag(ct_r[h]), blockdiag(ct_i[h])], axis=0) for h in halves])

    def cmul(x, y):
        return (x[0] * y[0] - x[1] * y[1], x[0] * y[1] + x[1] * y[0])

    a1 = (ar.reshape(-1), ai.reshape(-1))
    levels = []
    base = a1
    rowid = jnp.arange(SUBLANES)[:, None]
    for _ in range(3):
        p2 = cmul(base, base)
        p4 = cmul(p2, p2)
        carry = [base]
        for _i in range(SUBLANES - 1):
            carry.append(cmul(carry[-1], base))
        p8 = carry[-1]
        tabs = []
        for dshift, pw in ((1, base), (2, p2), (4, p4)):
            msk = (rowid >= dshift).astype(F32)
            tabs.append(jnp.concatenate([msk * pw[0][None, :], msk * pw[1][None, :]], axis=1))
        tabs.append(jnp.concatenate([jnp.stack([c[0] for c in carry]),
                                     jnp.stack([c[1] for c in carry])], axis=1))
        levels.append(jnp.stack(tabs))
        base = p8
    tab = jnp.stack(levels)
    return bblk, cblk, tab


def _gla_consts():
    n = GLA_CHUNK
    g = np.zeros((8, n, n), np.float32)
    valid = np.zeros((7, n, n), np.float32)
    rows = np.arange(n)
    for lev in range(6):
        s = n >> lev
        half = s // 2
        start = rows - rows % s
        bd = start + half - 1
        lower = (rows % s) < half
        for r in range(n):
            if lower[r]:
                g[lev, r, r + 1:bd[r] + 1] = 1.0
            else:
                g[lev, r, bd[r] + 1:r + 1] = 1.0
        same = start[:, None] == start[None, :]
        valid[lev] = same & (~lower)[:, None] & lower[None, :]
    valid[6] = np.eye(n)
    g[6] = np.tril(np.ones((n, n)))
    g[7] = rows[None, :] > rows[:, None]
    g = g.reshape(8 * n, n)
    g2 = np.concatenate([g, g], axis=1)
    valid_t = np.tile(valid, (1, 1, GLA_HEADS))
    heads = np.arange(GLA_HEADS)
    eye_h = heads[:, None] == heads[None, :]
    kmask = np.kron(eye_h, np.ones((n, GLA_DK)))
    vmask = np.kron(eye_h, np.ones((n, GLA_DV)))
    smask = np.kron(eye_h, np.ones((GLA_DV, GLA_DK)))
    return g2, valid_t, kmask, vmask, smask


def _gla_kernel(xn_ref, wq_ref, wk_ref, wv_ref, wgo_ref, wlr_ref, w2_ref, b2_ref, gn_ref,
                g2_ref, valid_ref, kmask_ref, vmask_ref, smask_ref, o_ref,
                q_s, k_s, v_s, go_s, la_s, st_s):
    rows = xn_ref.shape[0]
    n = GLA_CHUNK

    @pl.when(pl.program_id(1) == 0)
    def _():
        st_s[...] = jnp.zeros_like(st_s)

    xn = xn_ref[...]
    q_s[...] = _dot(xn, wq_ref[...]) * (GLA_DK ** -0.5)
    k_s[...] = _dot(xn, wk_ref[...])
    v_s[...] = _dot(xn, wv_ref[...]).astype(BF16)
    go_s[...] = _dot(xn, wgo_ref[...])
    lr = _dot(xn, wlr_ref[...])
    xg = _dot(lr.astype(BF16), w2_ref[...]) + b2_ref[...]
    la_s[...] = _log_sigmoid(xg) * (1.0 / GLA_GATE_NORM)

    chunks = range(rows // n)
    rsl = [pl.ds(ci * n, n) for ci in chunks]
    es = []
    for ci in chunks:
        la_hi, la_lo = _split_bf16(la_s[rsl[ci], :])
        es.append(jnp.exp(_dot(g2_ref[...], jnp.concatenate([la_hi, la_lo], axis=0))))

    atts = [jnp.zeros((n, GLA_HEADS * n), F32) for _ in chunks]
    for lev in range(7):
        for ci in chunks:
            q = q_s[rsl[ci], :]
            k = k_s[rsl[ci], :]
            if lev < 6:
                el = es[ci][lev * n:(lev + 1) * n]
                q = q * el
                k = k * el
            kblk = jnp.concatenate([k.astype(BF16)] * GLA_HEADS, axis=0) * kmask_ref[...]
            atts[ci] = atts[ci] + valid_ref[lev] * _dot_nt(q.astype(BF16), kblk)

    o_intra = []
    kv = []
    for ci in chunks:
        v = v_s[rsl[ci], :]
        vblk = jnp.concatenate([v] * GLA_HEADS, axis=0) * vmask_ref[...]
        o_intra.append(_dot(atts[ci].astype(BF16), vblk))
        khat = (k_s[rsl[ci], :] * es[ci][7 * n:8 * n]).astype(BF16)
        kv.append(smask_ref[...] * _dot_tn(v, khat))

    st = st_s[...]
    for ci in chunks:
        e_cum = es[ci][6 * n:7 * n]
        o = o_intra[ci] + _dot_nt((q_s[rsl[ci], :] * e_cum).astype(BF16), st.astype(BF16))
        st = st * e_cum[n - 1:n, :] + kv[ci]
        for h in range(GLA_HEADS):
            sl = slice(h * GLA_DV, (h + 1) * GLA_DV)
            oh = _rms(o[:, sl], gn_ref[...])
            o_ref[rsl[ci], sl] = (oh * _silu(go_s[rsl[ci], sl])).astype(o_ref.dtype)
    st_s[...] = st


def _gla(u, bsz, seq, layer, w_in, w2, b2, gn):
    n, d = u.shape
    rows = GLA_ROWS
    assert seq % rows == 0 and rows % GLA_CHUNK == 0
    nc = seq // rows
    g2, valid_t, kmask, vmask, smask = _gla_consts()
    consts = (jnp.asarray(g2, BF16), jnp.asarray(valid_t, F32), jnp.asarray(kmask, BF16),
              jnp.asarray(vmask, BF16), jnp.asarray(smask, F32))
    wspec = lambda name, width: _layer_spec(w_in, layer, 2, width, W_IN_COL[name])
    ls = lambda a: _layer_spec(a, layer, 2)
    return pl.pallas_call(
        _gla_kernel,
        grid=(bsz, nc),
        in_specs=[pl.BlockSpec((rows, d), lambda b, c: (b * nc + c, 0)),
                  wspec("q", GLA_KEY), wspec("k", GLA_KEY), wspec("v", GLA_VAL), wspec("go", GLA_VAL),
                  wspec("lr", LANES), ls(w2), ls(b2), ls(gn)]
        + [_const_spec(a, 2) for a in consts],
        out_specs=pl.BlockSpec((rows, GLA_VAL), lambda b, c: (b * nc + c, 0)),
        out_shape=jax.ShapeDtypeStruct((n, GLA_VAL), BF16),
        scratch_shapes=[pltpu.VMEM((rows, GLA_KEY), F32),
                        pltpu.VMEM((rows, GLA_KEY), F32),
                        pltpu.VMEM((rows, GLA_VAL), BF16),
                        pltpu.VMEM((rows, GLA_VAL), F32),
                        pltpu.VMEM((rows, GLA_KEY), F32),
                        pltpu.VMEM((GLA_HEADS * GLA_DV, GLA_KEY), F32)],
        compiler_params=_cparams(32, 2),
        name="gla",
    )(u, w_in, w_in, w_in, w_in, w_in, w2, b2, gn, *consts)


def _ssd_kernel(xn_ref, wz_ref, wx_ref, wdc_ref, wdr_ref,
                cw_ref, cb_ref, dbc_ref, dbr_ref, alc_ref, alr_ref, ale_ref,
                de_ref, gn_ref, tril_ref, triu_ref, hmask_ref, expand_ref, o_ref,
                buf_s, xbc_s, z_s, dte_s, lae_s, lac_s, lar_s, st_s):
    rows = xn_ref.shape[0]
    n = SSD_CHUNK
    tail = SUBLANES
    gw = SSD_INNER // SSD_GROUPS
    nslab = SSD_CONV_DIM // LANES

    @pl.when(pl.program_id(1) == 0)
    def _():
        st_s[...] = jnp.zeros_like(st_s)
        buf_s[:, 0:tail, :] = jnp.zeros((nslab, tail, LANES), F32)

    xn = xn_ref[...]
    z_s[...] = _dot(xn, wz_ref[...])
    xpre = _dot(xn, wx_ref[...])
    for j in range(nslab):
        buf_s[j, tail:tail + rows, :] = xpre[:, j * LANES:(j + 1) * LANES]
    for j in range(nslab):
        sl = slice(j * LANES, (j + 1) * LANES)
        conv = cb_ref[:, sl]
        for t in range(SSD_CONV):
            conv = conv + cw_ref[t:t + 1, sl] * buf_s[j, pl.ds(tail - (SSD_CONV - 1) + t, rows), :]
        xbc_s[:, sl] = _silu(conv)
    buf_s[:, 0:tail, :] = buf_s[:, rows:rows + tail, :]

    dtc = _softplus(_dot(xn, wdc_ref[...]) + dbc_ref[...])
    lac_s[...] = dtc * (-jnp.exp(alc_ref[...]))
    dtr = _softplus(_dot_nt(wdr_ref[...], xn) + dbr_ref[...])
    lar_s[...] = dtr * (-jnp.exp(alr_ref[...]))
    d_hi = dtc.astype(BF16)
    r1 = dtc - d_hi.astype(F32)
    d_mid = r1.astype(BF16)
    d_lo = (r1 - d_mid.astype(F32)).astype(BF16)
    dte = (_dot(d_hi, expand_ref[...]) + _dot(d_mid, expand_ref[...])) + _dot(d_lo, expand_ref[...])
    dte_s[...] = dte
    lae_s[...] = dte * (-jnp.exp(ale_ref[...]))

    causal = lax.broadcasted_iota(jnp.int32, (n, n), 0) >= lax.broadcasted_iota(jnp.int32, (n, n), 1)

    chunks = range(rows // n)
    rsl = [pl.ds(ci * n, n) for ci in chunks]
    groups = range(SSD_GROUPS)
    gsl = [slice(g * gw, (g + 1) * gw) for g in groups]
    tril = tril_ref[...]
    cum_c, cum_r, cum_e = [], [], []
    for ci in chunks:
        hi, lo = _split_bf16(lac_s[rsl[ci], :])
        cum_c.append(_dot(tril, hi) + _dot(tril, lo))
        hi, lo = _split_bf16(lar_s[:, rsl[ci]])
        cum_r.append(_dot(hi, triu_ref[...]) + _dot(lo, triu_ref[...]))
        hi, lo = _split_bf16(lae_s[rsl[ci], :])
        cum_e.append(_dot(tril, hi) + _dot(tril, lo))

    bms, cms, scs, kvs, xdts = [], [], [], [], []
    for ci in chunks:
        c_last = cum_e[ci][n - 1:n, :]
        xdt = xbc_s[rsl[ci], 0:SSD_INNER] * dte_s[rsl[ci], :]
        xw = (xdt * jnp.exp(c_last - cum_e[ci])).astype(BF16)
        xdts.append(xdt)
        bm = [xbc_s[rsl[ci], SSD_INNER + g * SSD_STATE:SSD_INNER + (g + 1) * SSD_STATE].astype(BF16)
              for g in groups]
        cm = [xbc_s[rsl[ci], SSD_INNER + (SSD_GROUPS + g) * SSD_STATE:
                    SSD_INNER + (SSD_GROUPS + g + 1) * SSD_STATE].astype(BF16) for g in groups]
        bms.append(bm)
        cms.append(cm)
        scs.append([_dot_nt(cm[g], bm[g]) for g in groups])
        kvs.append([_dot_tn(bm[g], xw[:, gsl[g]]) for g in groups])

    y_diag = []
    for ci in chunks:
        slabs = []
        for g in groups:
            for j in range(2):
                l0 = g * gw + j * LANES
                xp = xdts[ci][:, l0:l0 + LANES]
                yd = jnp.zeros((n, LANES), F32)
                for s in range(2):
                    hd = g * 4 + j * 2 + s
                    diff = cum_c[ci][:, hd:hd + 1] - cum_r[ci][hd:hd + 1, :]
                    dec = jnp.exp(jnp.where(causal, diff, -jnp.inf))
                    yd = yd + _dot((scs[ci][g] * dec).astype(BF16), (xp * hmask_ref[s]).astype(BF16))
                slabs.append(yd)
        y_diag.append(slabs)

    st = [st_s[:, gsl[g]] for g in groups]
    for ci in chunks:
        e_cum = jnp.exp(cum_e[ci])
        st_decay = jnp.exp(cum_e[ci][n - 1:n, :])
        xs = xbc_s[rsl[ci], 0:SSD_INNER]
        z = z_s[rsl[ci], :]
        for g in groups:
            y_off = _dot(cms[ci][g], st[g].astype(BF16)) * e_cum[:, gsl[g]]
            st[g] = st[g] * st_decay[:, gsl[g]] + kvs[ci][g]
            ys = []
            for j in range(2):
                l0 = g * gw + j * LANES
                y = (y_diag[ci][g * 2 + j] + y_off[:, j * LANES:(j + 1) * LANES]
                     + de_ref[:, l0:l0 + LANES] * xs[:, l0:l0 + LANES])
                ys.append(y * _silu(z[:, l0:l0 + LANES]))
            y = jnp.concatenate(ys, axis=1)
            o_ref[rsl[ci], gsl[g]] = _rms(y, gn_ref[:, gsl[g]]).astype(o_ref.dtype)
    for g in groups:
        st_s[:, gsl[g]] = st[g]


def _ssd(u, bsz, seq, layer, w_in, wdr, cw, cb, dbc, dbr, alc, alr, ale, de, gn):
    n, d = u.shape
    rows = SSD_ROWS
    assert seq % rows == 0 and rows % SSD_CHUNK == 0
    nc = seq // rows
    tril = np.tril(np.ones((SSD_CHUNK, SSD_CHUNK), np.float32))
    hmask = np.zeros((2, 1, LANES), np.float32)
    hmask[0, 0, :SSD_HEADDIM] = 1.0
    hmask[1, 0, SSD_HEADDIM:] = 1.0
    expand = np.zeros((LANES, SSD_INNER), np.float32)
    for hd in range(SSD_HEADS):
        expand[hd, hd * SSD_HEADDIM:(hd + 1) * SSD_HEADDIM] = 1.0
    params = (wdr, cw, cb, dbc, dbr, alc, alr, ale, de, gn)
    consts = (jnp.asarray(tril, BF16), jnp.asarray(tril.T, BF16), jnp.asarray(hmask),
              jnp.asarray(expand, BF16))
    wspec = lambda name, width: _layer_spec(w_in, layer, 2, width, W_IN_COL[name])
    return pl.pallas_call(
        _ssd_kernel,
        grid=(bsz, nc),
        in_specs=[pl.BlockSpec((rows, d), lambda b, c: (b * nc + c, 0)),
                  wspec("z", SSD_INNER), wspec("xbc", SSD_CONV_DIM), wspec("dt", LANES)]
        + [_layer_spec(a, layer, 2) for a in params] + [_const_spec(a, 2) for a in consts],
        out_specs=pl.BlockSpec((rows, SSD_INNER), lambda b, c: (b * nc + c, 0)),
        out_shape=jax.ShapeDtypeStruct((n, SSD_INNER), BF16),
        scratch_shapes=[pltpu.VMEM((SSD_CONV_DIM // LANES, rows + 2 * SUBLANES, LANES), F32),
                        pltpu.VMEM((rows, SSD_CONV_DIM), F32),
                        pltpu.VMEM((rows, SSD_INNER), F32),
                        pltpu.VMEM((rows, SSD_INNER), F32),
                        pltpu.VMEM((rows, SSD_INNER), F32),
                        pltpu.VMEM((rows, LANES), F32),
                        pltpu.VMEM((SUBLANES, rows), F32),
                        pltpu.VMEM((SSD_STATE, SSD_INNER), F32)],
        compiler_params=_cparams(32, 2),
        name="ssd",
    )(u, w_in, w_in, w_in, *params, *consts)


def _merge_kernel(h_ref, xn_ref, wgate_ref, y0_ref, y1_ref, y2_ref, w0_ref, w1_ref, w2_ref,
                  wout_ref, o_ref):
    xn = xn_ref[...]
    d = h_ref.shape[1]
    merged = jnp.zeros(h_ref.shape, F32)
    for i, (y_ref, w_ref) in enumerate(((y0_ref, w0_ref), (y1_ref, w1_ref), (y2_ref, w2_ref))):
        gate = jax.nn.sigmoid(_dot(xn, wgate_ref[:, i * d:(i + 1) * d]))
        merged = merged + gate * _dot(y_ref[...], w_ref[...])
    o_ref[...] = h_ref[...] + _dot(merged.astype(BF16), wout_ref[...])


def _merge(h, xn, layer, w_in, ys, wbr, wout):
    n, d = h.shape
    rows = min(MERGE_ROWS, n)
    row_spec = pl.BlockSpec((rows, d), lambda i: (i, 0))
    y_specs = [pl.BlockSpec((rows, y.shape[1]), lambda i: (i, 0)) for y in ys]
    return pl.pallas_call(
        _merge_kernel,
        grid=(n // rows,),
        in_specs=[row_spec, row_spec, _layer_spec(w_in, layer, 1, 3 * d, W_IN_COL["gates"])] + y_specs
        + [_layer_spec(w, layer, 1) for w in wbr] + [_layer_spec(wout, layer, 1)],
        out_specs=row_spec,
        out_shape=jax.ShapeDtypeStruct((n, d), F32),
        compiler_params=_cparams(40, 1),
        name="merge",
    )(h, xn, w_in, *ys, *wbr, wout)


def _rows(v):
    return v.astype(F32)[:, None, :]


def _pad_last(w, width):
    return jnp.pad(w, [(0, 0)] * (w.ndim - 1) + [(0, width - w.shape[-1])])


def _pack_w_in(w_in):
    o = _IN_OFFS
    return jnp.concatenate([_pad_last(w_in[..., o[s0]:o[s1]], width)
                            for _, width, s0, s1 in _W_IN_PACK], axis=-1).astype(BF16)


def kernel(x, p, ffn1_norm, ffn1_gate, ffn1_up, ffn1_down, mix_norm, w_in, s5_lam_re, s5_lam_im, s5_log_step, s5_b_re, s5_b_im, s5_c_re, s5_c_im, s5_d, s5_glu, gla_gate_w2, gla_gate_b2, gla_norm, ssd_conv_w, ssd_conv_b, ssd_dt_bias, ssd_a_log, ssd_d, ssd_norm, w_br_s5, w_br_gla, w_br_ssd, w_out, ffn2_norm, ffn2_gate, ffn2_up, ffn2_down, ple_norm, ple_gate, ple_proj, final_norm):
    bsz, seq, d = x.shape
    depth = p.shape[0]
    n = bsz * seq
    bf = lambda w: w.astype(BF16)

    ffn1 = (_rows(ffn1_norm), bf(ffn1_gate), bf(ffn1_up), bf(ffn1_down))
    ffn2 = (_rows(ffn2_norm), bf(ffn2_gate), bf(ffn2_up), bf(ffn2_down))
    w_in_p = _pack_w_in(w_in)
    bblk, cblk, tab = jax.vmap(_s5_params)(s5_lam_re, s5_lam_im, s5_log_step, s5_b_re, s5_b_im,
                                           s5_c_re, s5_c_im)
    o = _IN_OFFS
    rep = SSD_HEADDIM
    gla_params = (bf(jnp.pad(gla_gate_w2, ((0, 0), (0, LANES - GLA_RANK), (0, 0)))),
                  _rows(gla_gate_b2), _rows(gla_norm))
    ssd_params = (bf(jnp.swapaxes(w_in[..., o[8]:o[9]], 1, 2)),
                  ssd_conv_w.astype(F32), _rows(ssd_conv_b),
                  _rows(_pad_last(ssd_dt_bias, LANES)), ssd_dt_bias.astype(F32)[:, :, None],
                  _rows(_pad_last(ssd_a_log, LANES)), ssd_a_log.astype(F32)[:, :, None],
                  _rows(jnp.repeat(ssd_a_log, rep, axis=-1)), _rows(jnp.repeat(ssd_d, rep, axis=-1)),
                  _rows(ssd_norm))
    branch_w = (bf(w_br_s5), bf(w_br_gla), bf(w_br_ssd))
    w_out_b = bf(w_out)
    mixn = _rows(mix_norm)
    s5_dskip = _rows(s5_d)
    s5_glu_b = bf(s5_glu)
    ple = (_rows(ple_norm), bf(ple_gate))
    p_rows = p.reshape(depth, n, p.shape[-1])
    ple_proj_b = bf(ple_proj)
    fn = final_norm.astype(F32).reshape(1, -1)

    h = x.reshape(n, d)
    for i in range(depth):
        h, u, y_s5 = _ffn_s5(h, seq, i, *ffn1, mixn, w_in_p, bblk, cblk, tab, s5_dskip, s5_glu_b)
        y_gla = _gla(u, bsz, seq, i, w_in_p, *gla_params)
        y_ssd = _ssd(u, bsz, seq, i, w_in_p, *ssd_params)
        h = _merge(h, u, i, w_in_p, (y_s5, y_gla, y_ssd), branch_w, w_out_b)
        h = _ffn_ple(h, i, *ffn2, *ple, p_rows, ple_proj_b, fn, i == depth - 1)
    return h.reshape(bsz, seq, d)
```

```python
import functools

import numpy as np
import jax
import jax.numpy as jnp
from jax import lax
from jax.experimental import pallas as pl
from jax.experimental.pallas import tpu as pltpu

F32 = jnp.float32
BF16 = jnp.bfloat16

EPS = 1e-6
D_MODEL = 1024
D_FF = 2752
PLE_DIM = 256

S5_WIDTH = 512
S5_GROUP = 16
S5_GROUPS = 32
S5_STATE = 64
S5_NSTATE = S5_GROUPS * S5_STATE

GLA_HEADS = 4
GLA_DK = 64
GLA_DV = 128
GLA_KEY = 256
GLA_VAL = 512
GLA_RANK = 16
GLA_GATE_NORM = 16.0

SSD_HEADS = 8
SSD_HEADDIM = 64
SSD_INNER = 512
SSD_GROUPS = 2
SSD_STATE = 128
SSD_CONV = 4
SSD_CONV_DIM = 1024

_IN_SIZES = (S5_WIDTH, GLA_KEY, GLA_KEY, GLA_VAL, GLA_VAL, GLA_RANK,
             SSD_INNER, SSD_CONV_DIM, SSD_HEADS, D_MODEL, D_MODEL, D_MODEL)
_IN_OFFS = tuple(int(v) for v in np.concatenate([[0], np.cumsum(_IN_SIZES)]))

_W_IN_PACK = (("gates", 3 * D_MODEL, 9, 12), ("xbc", SSD_CONV_DIM, 7, 8), ("s5", S5_WIDTH, 0, 1),
              ("v", GLA_VAL, 3, 4), ("go", GLA_VAL, 4, 5), ("z", SSD_INNER, 6, 7),
              ("q", GLA_KEY, 1, 2), ("k", GLA_KEY, 2, 3), ("lr", 128, 5, 6), ("dt", 128, 8, 9))
W_IN_COL = {}
_off = 0
for _name, _width, _s0, _s1 in _W_IN_PACK:
    assert _off % _width == 0
    W_IN_COL[_name] = _off // _width
    _off += _width

LANES = 128
SUBLANES = 8

FFN_ROWS = 512
FFN_PIECES = ((0, 11 * LANES), (11 * LANES, D_FF))
S5_ROWS = 512
FFN_CHUNK = 256
GLA_ROWS = 256
GLA_CHUNK = 64
SSD_ROWS = 1024
SSD_CHUNK = 128
MERGE_ROWS = 512


def _cparams(vmem_mib, ndims):
    return pltpu.CompilerParams(
        dimension_semantics=("arbitrary",) * ndims,
        vmem_limit_bytes=vmem_mib * 1024 * 1024)


def _const_spec(arr, ngrid):
    zeros = (0,) * arr.ndim
    if ngrid == 1:
        imap = lambda i: zeros
    else:
        imap = lambda b, c: zeros
    return pl.BlockSpec(arr.shape, imap, pipeline_mode=pl.Buffered(1))


def _layer_spec(arr, layer, ngrid, width=None, col=0):
    shape = (None,) + arr.shape[1:]
    idx = (layer,) + (0,) * (arr.ndim - 1)
    if width is not None:
        shape = shape[:-1] + (width,)
        idx = idx[:-1] + (col,)
    if ngrid == 1:
        imap = lambda i: idx
    else:
        imap = lambda b, c: idx
    return pl.BlockSpec(shape, imap, pipeline_mode=pl.Buffered(1))


def _rms(x, w):
    ms = jnp.mean(x * x, axis=-1, keepdims=True)
    return x * lax.rsqrt(ms + EPS) * w


def _silu(x):
    return x * jax.nn.sigmoid(x)


def _softplus(x):
    return jnp.maximum(x, 0.0) + jnp.log1p(jnp.exp(-jnp.abs(x)))


def _log_sigmoid(x):
    return jnp.minimum(x, 0.0) - jnp.log1p(jnp.exp(-jnp.abs(x)))


def _dot(a, b):
    return jnp.dot(a, b, preferred_element_type=F32)


def _dot_nt(a, b):
    return lax.dot_general(a, b, (((1,), (1,)), ((), ())), preferred_element_type=F32)


def _dot_tn(a, b):
    return lax.dot_general(a, b, (((0,), (0,)), ((), ())), preferred_element_type=F32)


def _split_bf16(x):
    hi = x.astype(BF16)
    lo = (x - hi.astype(F32)).astype(BF16)
    return hi, lo


def _ffn_core(x, nw_ref, wg_ref, wu_ref, wd_ref):
    xn = _rms(x, nw_ref[...]).astype(BF16)
    acc = jnp.zeros_like(x)
    for lo, hi in FFN_PIECES:
        g = _dot(xn, wg_ref[:, lo:hi])
        u = _dot(xn, wu_ref[:, lo:hi])
        hid = (_silu(g) * u).astype(BF16)
        acc = acc + _dot(hid, wd_ref[lo:hi, :])
    return x + 0.5 * acc


def _interleave(main, side):
    done = 0
    for i, step in enumerate(main):
        step()
        upto = (i + 1) * len(side) // len(main)
        for s in side[done:upto]:
            s()
        done = upto


def _ffn_s5_kernel(h_ref, nw_ref, wg_ref, wu_ref, wd_ref, nw2_ref,
                   win_ref, bblk_ref, cblk_ref, tab_ref, d_ref, wglu_ref,
                   o_ref, u_ref, y_ref,
                   ukeep_s, x_s, e1_s, e2_s, st_s, *, tiles_per_seq):
    i = pl.program_id(0)

    @pl.when(i == 0)
    def _():
        ukeep_s[...] = jnp.zeros_like(ukeep_s)

    @pl.when(jnp.logical_or(i == 0, lax.rem(i + tiles_per_seq - 1, tiles_per_seq) == 0))
    def _():
        st_s[...] = jnp.zeros_like(st_s)

    rows = h_ref.shape[0]
    n1 = rows // SUBLANES
    n2 = n1 // SUBLANES
    u5 = _s5_project(ukeep_s[...], win_ref, bblk_ref, x_s)

    P = functools.partial
    scan = [P(_s5_scan_block, x_s, k, tab_ref, 0, e1_s) for k in range(n1)]
    scan += [P(_s5_scan_block, e1_s, k, tab_ref, 1, e2_s) for k in range(n2)]
    scan += [P(_s5_scan_block, e2_s, 0, tab_ref, 2, None),
             P(_s5_fix_block, e2_s, 0, tab_ref, 2, st_s, 0),
             P(_s5_fix_block, e1_s, 0, tab_ref, 1, st_s, 0)]
    scan += [P(_s5_fix_block, e1_s, k, tab_ref, 1, e2_s, k - 1) for k in range(1, n2)]
    scan += [P(_s5_fix_block, x_s, 0, tab_ref, 0, st_s, 0)]
    scan += [P(_s5_fix_block, x_s, k, tab_ref, 0, e1_s, k - 1) for k in range(1, n1)]

    x = h_ref[...]
    xn = _rms(x, nw_ref[...]).astype(BF16)
    acc = [jnp.zeros_like(x)]

    def ffn_chunk(c):
        sl = slice(c * FFN_CHUNK, min((c + 1) * FFN_CHUNK, D_FF))
        hid = (_silu(_dot(xn, wg_ref[:, sl])) * _dot(xn, wu_ref[:, sl])).astype(BF16)
        acc[0] = acc[0] + _dot(hid, wd_ref[sl, :])

    nchunks = -(-D_FF // FFN_CHUNK)
    _interleave([P(ffn_chunk, c) for c in range(nchunks)], scan)
    st_s[...] = jnp.broadcast_to(e2_s[SUBLANES - 1:SUBLANES, :], st_s.shape)

    y = x + 0.5 * acc[0]
    o_ref[...] = y
    u = _rms(y, nw2_ref[...]).astype(BF16)
    u_ref[...] = u
    ukeep_s[...] = u
    y_ref[...] = _s5_output(u5, cblk_ref, d_ref, wglu_ref, x_s).astype(y_ref.dtype)


def _ffn_s5(h, seq, layer, nw, wg, wu, wd, nw2, w_in, bblk, cblk, tab, dskip, wglu):
    n, d = h.shape
    rows = S5_ROWS
    assert rows == SUBLANES ** 3 and seq % rows == 0 and n % seq == 0
    nt = n // rows
    cur = lambda i: (jnp.minimum(i, nt - 1), 0)
    prev = lambda i: (jnp.maximum(i - 1, 0), 0)
    ls = lambda a: _layer_spec(a, layer, 1)
    return pl.pallas_call(
        functools.partial(_ffn_s5_kernel, tiles_per_seq=seq // rows),
        grid=(nt + 1,),
        in_specs=[pl.BlockSpec((rows, d), cur), ls(nw), ls(wg), ls(wu), ls(wd), ls(nw2),
                  _layer_spec(w_in, layer, 1, S5_WIDTH, W_IN_COL["s5"]),
                  ls(bblk), ls(cblk), ls(tab), ls(dskip), ls(wglu)],
        out_specs=(pl.BlockSpec((rows, d), cur), pl.BlockSpec((rows, d), cur),
                   pl.BlockSpec((rows, S5_WIDTH), prev)),
        out_shape=(jax.ShapeDtypeStruct((n, d), F32), jax.ShapeDtypeStruct((n, d), BF16),
                   jax.ShapeDtypeStruct((n, S5_WIDTH), BF16)),
        scratch_shapes=[pltpu.VMEM((rows, d), BF16),
                        pltpu.VMEM((rows, 2 * S5_NSTATE), F32),
                        pltpu.VMEM((rows // SUBLANES, 2 * S5_NSTATE), F32),
                        pltpu.VMEM((SUBLANES, 2 * S5_NSTATE), F32),
                        pltpu.VMEM((SUBLANES, 2 * S5_NSTATE), F32)],
        compiler_params=_cparams(60, 1),
        name="ffn_s5",
    )(h, nw, wg, wu, wd, nw2, w_in, bblk, cblk, tab, dskip, wglu)


def _ffn_ple_kernel(h_ref, nw_ref, wg_ref, wu_ref, wd_ref, pn_ref, wpg_ref, p_ref, wpp_ref, fn_ref,
                    o_ref, *, final):
    y = _ffn_core(h_ref[...], nw_ref, wg_ref, wu_ref, wd_ref)
    gate = jax.nn.sigmoid(_dot(_rms(y, pn_ref[...]).astype(BF16), wpg_ref[...]))
    y = y + gate * _dot(p_ref[...].astype(BF16), wpp_ref[...])
    if final:
        y = _rms(y, fn_ref[...])
    o_ref[...] = y


def _ffn_ple(h, layer, nw, wg, wu, wd, pn, wpg, p, wpp, fn, final):
    n, d = h.shape
    rows = min(FFN_ROWS, n)
    row_spec = pl.BlockSpec((rows, d), lambda i: (i, 0))
    ls = lambda a: _layer_spec(a, layer, 1)
    return pl.pallas_call(
        functools.partial(_ffn_ple_kernel, final=final),
        grid=(n // rows,),
        in_specs=[row_spec, ls(nw), ls(wg), ls(wu), ls(wd), ls(pn), ls(wpg),
                  pl.BlockSpec((None, rows, p.shape[2]), lambda i: (layer, i, 0)), ls(wpp),
                  _const_spec(fn, 1)],
        out_specs=row_spec,
        out_shape=jax.ShapeDtypeStruct((n, d), F32),
        compiler_params=_cparams(56, 1),
        name="ffn_ple",
    )(h, nw, wg, wu, wd, pn, wpg, p, wpp, fn)


def _s5_scan_block(r_ref, k, tab_ref, lev, e_ref):
    p = S5_NSTATE
    r0 = k * SUBLANES
    xr = r_ref[pl.ds(r0, SUBLANES), 0:p]
    xi = r_ref[pl.ds(r0, SUBLANES), p:2 * p]
    for j, d in enumerate((1, 2, 4)):
        pr = tab_ref[lev, j, :, 0:p]
        pi = tab_ref[lev, j, :, p:2 * p]
        sr = pltpu.roll(xr, d, 0)
        si = pltpu.roll(xi, d, 0)
        xr, xi = xr + pr * sr - pi * si, xi + pr * si + pi * sr
    r_ref[pl.ds(r0, SUBLANES), 0:p] = xr
    r_ref[pl.ds(r0, SUBLANES), p:2 * p] = xi
    if e_ref is not None:
        e_ref[pl.ds(k, 1), 0:p] = xr[SUBLANES - 1:SUBLANES, :]
        e_ref[pl.ds(k, 1), p:2 * p] = xi[SUBLANES - 1:SUBLANES, :]


def _s5_fix_block(r_ref, k, tab_ref, lev, src_ref, row):
    p = S5_NSTATE
    pr = src_ref[pl.ds(row, 1), 0:p]
    pi = src_ref[pl.ds(row, 1), p:2 * p]
    r0 = k * SUBLANES
    cr = tab_ref[lev, 3, :, 0:p]
    ci = tab_ref[lev, 3, :, p:2 * p]
    xr = r_ref[pl.ds(r0, SUBLANES), 0:p]
    xi = r_ref[pl.ds(r0, SUBLANES), p:2 * p]
    r_ref[pl.ds(r0, SUBLANES), 0:p] = xr + cr * pr - ci * pi
    r_ref[pl.ds(r0, SUBLANES), p:2 * p] = xi + cr * pi + ci * pr


def _s5_project(xn, win_ref, bblk_ref, x_ref):
    u = _dot(xn, win_ref[...])
    ub = u.astype(BF16)
    hw = S5_WIDTH // 2
    hp = S5_NSTATE // 2
    for j in range(2):
        xj = _dot(ub[:, j * hw:(j + 1) * hw], bblk_ref[j])
        x_ref[:, j * hp:(j + 1) * hp] = xj[:, 0:hp]
        x_ref[:, S5_NSTATE + j * hp:S5_NSTATE + (j + 1) * hp] = xj[:, hp:2 * hp]
    return u


def _s5_output(u, cblk_ref, d_ref, wglu_ref, x_ref):
    hp = S5_NSTATE // 2
    ys = []
    for j in range(2):
        hr = x_ref[:, j * hp:(j + 1) * hp].astype(BF16)
        hi = x_ref[:, S5_NSTATE + j * hp:S5_NSTATE + (j + 1) * hp].astype(BF16)
        ys.append(_dot(hr, cblk_ref[j, 0:hp, :]) + _dot(hi, cblk_ref[j, hp:2 * hp, :]))
    y = jnp.concatenate(ys, axis=1) + d_ref[...] * u
    g = jax.nn.gelu(y, approximate=True)
    z = _dot(g.astype(BF16), wglu_ref[...])
    return g * jax.nn.sigmoid(z)


def _s5_params(lam_re, lam_im, log_step, b_re, b_im, c_re, c_im):
    lr = jnp.minimum(lam_re.astype(F32), -1e-4)
    li = lam_im.astype(F32)
    step = jnp.exp(log_step.astype(F32))[:, None]
    mag = jnp.exp(lr * step)
    ar = mag * jnp.cos(li * step)
    ai = mag * jnp.sin(li * step)
    den = lr * lr + li * li
    nr = ar - 1.0
    fr = (nr * lr + ai * li) / den
    fi = (ai * lr - nr * li) / den
    br = b_re.astype(F32)
    bi = b_im.astype(F32)
    bbr = fr[..., None] * br - fi[..., None] * bi
    bbi = fr[..., None] * bi + fi[..., None] * br
    gh = S5_GROUPS // 2
    eye = jnp.eye(gh, dtype=BF16)

    def blockdiag(m):
        g, r, c = m.shape
        return (eye[:, None, :, None] * m.astype(BF16)[:, :, None, :]).reshape(g * r, g * c)

    halves = [slice(j * gh, (j + 1) * gh) for j in range(2)]
    bt_r, bt_i = bbr.transpose(0, 2, 1), bbi.transpose(0, 2, 1)
    ct_r, ct_i = c_re.astype(F32).transpose(0, 2, 1), -c_im.astype(F32).transpose(0, 2, 1)
    bblk = jnp.stack([jnp.concatenate([blockdiag(bt_r[h]), blockdiag(bt_i[h])], axis=1) for h in halves])
    cblk = jnp.stack([jnp.concatenate([blockdiag(ct_r[h]), blockdiag(ct_i[h])], axis=0) for h in halves])

    def cmul(x, y):
        return (x[0] * y[0] - x[1] * y[1], x[0] * y[1] + x[1] * y[0])

    a1 = (ar.reshape(-1), ai.reshape(-1))
    levels = []
    base = a1
    rowid = jnp.arange(SUBLANES)[:, None]
    for _ in range(3):
        p2 = cmul(base, base)
        p4 = cmul(p2, p2)
        carry = [base]
        for _i in range(SUBLANES - 1):
            carry.append(cmul(carry[-1], base))
        p8 = carry[-1]
        tabs = []
        for dshift, pw in ((1, base), (2, p2), (4, p4)):
            msk = (rowid >= dshift).astype(F32)
            tabs.append(jnp.concatenate([msk * pw[0][None, :], msk * pw[1][None, :]], axis=1))
        tabs.append(jnp.concatenate([jnp.stack([c[0] for c in carry]),
                                     jnp.stack([c[1] for c in carry])], axis=1))
        levels.append(jnp.stack(tabs))
        base = p8
    tab = jnp.stack(levels)
    return bblk, cblk, tab


def _gla_consts():
    n = GLA_CHUNK
    g = np.zeros((8, n, n), np.float32)
    valid = np.zeros((7, n, n), np.float32)
    rows = np.arange(n)
    for lev in range(6):
        s = n >> lev
        half = s // 2
        start = rows - rows % s
        bd = start + half - 1
        lower = (rows % s) < half
        for r in range(n):
            if lower[r]:
                g[lev, r, r + 1:bd[r] + 1] = 1.0
            else:
                g[lev, r, bd[r] + 1:r + 1] = 1.0
        same = start[:, None] == start[None, :]
        valid[lev] = same & (~lower)[:, None] & lower[None, :]
    valid[6] = np.eye(n)
    g[6] = np.tril(np.ones((n, n)))
    g[7] = rows[None, :] > rows[:, None]
    g = g.reshape(8 * n, n)
    g2 = np.concatenate([g, g], axis=1)
    valid_t = np.tile(valid, (1, 1, GLA_HEADS))
    heads = np.arange(GLA_HEADS)
    eye_h = heads[:, None] == heads[None, :]
    kmask = np.kron(eye_h, np.ones((n, GLA_DK)))
    vmask = np.kron(eye_h, np.ones((n, GLA_DV)))
    smask = np.kron(eye_h, np.ones((GLA_DV, GLA_DK)))
    return g2, valid_t, kmask, vmask, smask


def _gla_kernel(xn_ref, wq_ref, wk_ref, wv_ref, wgo_ref, wlr_ref, w2_ref, b2_ref, gn_ref,
                g2_ref, valid_ref, kmask_ref, vmask_ref, smask_ref, o_ref,
                q_s, k_s, v_s, go_s, la_s, st_s):
    rows = xn_ref.shape[0]
    n = GLA_CHUNK

    @pl.when(pl.program_id(1) == 0)
    def _():
        st_s[...] = jnp.zeros_like(st_s)

    xn = xn_ref[...]
    q_s[...] = _dot(xn, wq_ref[...]) * (GLA_DK ** -0.5)
    k_s[...] = _dot(xn, wk_ref[...])
    v_s[...] = _dot(xn, wv_ref[...]).astype(BF16)
    go_s[...] = _dot(xn, wgo_ref[...])
    lr = _dot(xn, wlr_ref[...])
    xg = _dot(lr.astype(BF16), w2_ref[...]) + b2_ref[...]
    la_s[...] = _log_sigmoid(xg) * (1.0 / GLA_GATE_NORM)

    chunks = range(rows // n)
    rsl = [pl.ds(ci * n, n) for ci in chunks]
    es = []
    for ci in chunks:
        la_hi, la_lo = _split_bf16(la_s[rsl[ci], :])
        es.append(jnp.exp(_dot(g2_ref[...], jnp.concatenate([la_hi, la_lo], axis=0))))

    atts = [jnp.zeros((n, GLA_HEADS * n), F32) for _ in chunks]
    for lev in range(7):
        for ci in chunks:
            q = q_s[rsl[ci], :]
            k = k_s[rsl[ci], :]
            if lev < 6:
                el = es[ci][lev * n:(lev + 1) * n]
                q = q * el
                k = k * el
            kblk = jnp.concatenate([k.astype(BF16)] * GLA_HEADS, axis=0) * kmask_ref[...]
            atts[ci] = atts[ci] + valid_ref[lev] * _dot_nt(q.astype(BF16), kblk)

    o_intra = []
    kv = []
    for ci in chunks:
        v = v_s[rsl[ci], :]
        vblk = jnp.concatenate([v] * GLA_HEADS, axis=0) * vmask_ref[...]
        o_intra.append(_dot(atts[ci].astype(BF16), vblk))
        khat = (k_s[rsl[ci], :] * es[ci][7 * n:8 * n]).astype(BF16)
        kv.append(smask_ref[...] * _dot_tn(v, khat))

    st = st_s[...]
    for ci in chunks:
        e_cum = es[ci][6 * n:7 * n]
        o = o_intra[ci] + _dot_nt((q_s[rsl[ci], :] * e_cum).astype(BF16), st.astype(BF16))
        st = st * e_cum[n - 1:n, :] + kv[ci]
        for h in range(GLA_HEADS):
            sl = slice(h * GLA_DV, (h + 1) * GLA_DV)
            oh = _rms(o[:, sl], gn_ref[...])
            o_ref[rsl[ci], sl] = (oh * _silu(go_s[rsl[ci], sl])).astype(o_ref.dtype)
    st_s[...] = st


def _gla(u, bsz, seq, layer, w_in, w2, b2, gn):
    n, d = u.shape
    rows = GLA_ROWS
    assert seq % rows == 0 and rows % GLA_CHUNK == 0
    nc = seq // rows
    g2, valid_t, kmask, vmask, smask = _gla_consts()
    consts = (jnp.asarray(g2, BF16), jnp.asarray(valid_t, F32), jnp.asarray(kmask, BF16),
              jnp.asarray(vmask, BF16), jnp.asarray(smask, F32))
    wspec = lambda name, width: _layer_spec(w_in, layer, 2, width, W_IN_COL[name])
    ls = lambda a: _layer_spec(a, layer, 2)
    return pl.pallas_call(
        _gla_kernel,
        grid=(bsz, nc),
        in_specs=[pl.BlockSpec((rows, d), lambda b, c: (b * nc + c, 0)),
                  wspec("q", GLA_KEY), wspec("k", GLA_KEY), wspec("v", GLA_VAL), wspec("go", GLA_VAL),
                  wspec("lr", LANES), ls(w2), ls(b2), ls(gn)]
        + [_const_spec(a, 2) for a in consts],
        out_specs=pl.BlockSpec((rows, GLA_VAL), lambda b, c: (b * nc + c, 0)),
        out_shape=jax.ShapeDtypeStruct((n, GLA_VAL), BF16),
        scratch_shapes=[pltpu.VMEM((rows, GLA_KEY), F32),
                        pltpu.VMEM((rows, GLA_KEY), F32),
                        pltpu.VMEM((rows, GLA_VAL), BF16),
                        pltpu.VMEM((rows, GLA_VAL), F32),
                        pltpu.VMEM((rows, GLA_KEY), F32),
                        pltpu.VMEM((GLA_HEADS * GLA_DV, GLA_KEY), F32)],
        compiler_params=_cparams(32, 2),
        name="gla",
    )(u, w_in, w_in, w_in, w_in, w_in, w2, b2, gn, *consts)


def _ssd_kernel(xn_ref, wz_ref, wx_ref, wdc_ref, wdr_ref,
                cw_ref, cb_ref, dbc_ref, dbr_ref, alc_ref, alr_ref, ale_ref,
                de_ref, gn_ref, tril_ref, triu_ref, hmask_ref, expand_ref, o_ref,
                buf_s, xbc_s, z_s, dte_s, lae_s, lac_s, lar_s, st_s):
    rows = xn_ref.shape[0]
    n = SSD_CHUNK
    tail = SUBLANES
    gw = SSD_INNER // SSD_GROUPS
    nslab = SSD_CONV_DIM // LANES

    @pl.when(pl.program_id(1) == 0)
    def _():
        st_s[...] = jnp.zeros_like(st_s)
        buf_s[:, 0:tail, :] = jnp.zeros((nslab, tail, LANES), F32)

    xn = xn_ref[...]
    z_s[...] = _dot(xn, wz_ref[...])
    xpre = _dot(xn, wx_ref[...])
    for j in range(nslab):
        buf_s[j, tail:tail + rows, :] = xpre[:, j * LANES:(j + 1) * LANES]
    for j in range(nslab):
        sl = slice(j * LANES, (j + 1) * LANES)
        conv = cb_ref[:, sl]
        for t in range(SSD_CONV):
            conv = conv + cw_ref[t:t + 1, sl] * buf_s[j, pl.ds(tail - (SSD_CONV - 1) + t, rows), :]
        xbc_s[:, sl] = _silu(conv)
    buf_s[:, 0:tail, :] = buf_s[:, rows:rows + tail, :]

    dtc = _softplus(_dot(xn, wdc_ref[...]) + dbc_ref[...])
    lac_s[...] = dtc * (-jnp.exp(alc_ref[...]))
    dtr = _softplus(_dot_nt(wdr_ref[...], xn) + dbr_ref[...])
    lar_s[...] = dtr * (-jnp.exp(alr_ref[...]))
    d_hi = dtc.astype(BF16)
    r1 = dtc - d_hi.astype(F32)
    d_mid = r1.astype(BF16)
    d_lo = (r1 - d_mid.astype(F32)).astype(BF16)
    dte = (_dot(d_hi, expand_ref[...]) + _dot(d_mid, expand_ref[...])) + _dot(d_lo, expand_ref[...])
    dte_s[...] = dte
    lae_s[...] = dte * (-jnp.exp(ale_ref[...]))

    causal = lax.broadcasted_iota(jnp.int32, (n, n), 0) >= lax.broadcasted_iota(jnp.int32, (n, n), 1)

    chunks = range(rows // n)
    rsl = [pl.ds(ci * n, n) for ci in chunks]
    groups = range(SSD_GROUPS)
    gsl = [slice(g * gw, (g + 1) * gw) for g in groups]
    tril = tril_ref[...]
    cum_c, cum_r, cum_e = [], [], []
    for ci in chunks:
        hi, lo = _split_bf16(lac_s[rsl[ci], :])
        cum_c.append(_dot(tril, hi) + _dot(tril, lo))
        hi, lo = _split_bf16(lar_s[:, rsl[ci]])
        cum_r.append(_dot(hi, triu_ref[...]) + _dot(lo, triu_ref[...]))
        hi, lo = _split_bf16(lae_s[rsl[ci], :])
        cum_e.append(_dot(tril, hi) + _dot(tril, lo))

    bms, cms, scs, kvs, xdts = [], [], [], [], []
    for ci in chunks:
        c_last = cum_e[ci][n - 1:n, :]
        xdt = xbc_s[rsl[ci], 0:SSD_INNER] * dte_s[rsl[ci], :]
        xw = (xdt * jnp.exp(c_last - cum_e[ci])).astype(BF16)
        xdts.append(xdt)
        bm = [xbc_s[rsl[ci], SSD_INNER + g * SSD_STATE:SSD_INNER + (g + 1) * SSD_STATE].astype(BF16)
              for g in groups]
        cm = [xbc_s[rsl[ci], SSD_INNER + (SSD_GROUPS + g) * SSD_STATE:
                    SSD_INNER + (SSD_GROUPS + g + 1) * SSD_STATE].astype(BF16) for g in groups]
        bms.append(bm)
        cms.append(cm)
        scs.append([_dot_nt(cm[g], bm[g]) for g in groups])
        kvs.append([_dot_tn(bm[g], xw[:, gsl[g]]) for g in groups])

    y_diag = []
    for ci in chunks:
        slabs = []
        for g in groups:
            for j in range(2):
                l0 = g * gw + j * LANES
                xp = xdts[ci][:, l0:l0 + LANES]
                yd = jnp.zeros((n, LANES), F32)
                for s in range(2):
                    hd = g * 4 + j * 2 + s
                    diff = cum_c[ci][:, hd:hd + 1] - cum_r[ci][hd:hd + 1, :]
                    dec = jnp.exp(jnp.where(causal, diff, -jnp.inf))
                    yd = yd + _dot((scs[ci][g] * dec).astype(BF16), (xp * hmask_ref[s]).astype(BF16))
                slabs.append(yd)
        y_diag.append(slabs)

    st = [st_s[:, gsl[g]] for g in groups]
    for ci in chunks:
        e_cum = jnp.exp(cum_e[ci])
        st_decay = jnp.exp(cum_e[ci][n - 1:n, :])
        xs = xbc_s[rsl[ci], 0:SSD_INNER]
        z = z_s[rsl[ci], :]
        for g in groups:
            y_off = _dot(cms[ci][g], st[g].astype(BF16)) * e_cum[:, gsl[g]]
            st[g] = st[g] * st_decay[:, gsl[g]] + kvs[ci][g]
            ys = []
            for j in range(2):
                l0 = g * gw + j * LANES
                y = (y_diag[ci][g * 2 + j] + y_off[:, j * LANES:(j + 1) * LANES]
                     + de_ref[:, l0:l0 + LANES] * xs[:, l0:l0 + LANES])
                ys.append(y * _silu(z[:, l0:l0 + LANES]))
            y = jnp.concatenate(ys, axis=1)
            o_ref[rsl[ci], gsl[g]] = _rms(y, gn_ref[:, gsl[g]]).astype(o_ref.dtype)
    for g in groups:
        st_s[:, gsl[g]] = st[g]


def _ssd(u, bsz, seq, layer, w_in, wdr, cw, cb, dbc, dbr, alc, alr, ale, de, gn):
    n, d = u.shape
    rows = SSD_ROWS
    assert seq % rows == 0 and rows % SSD_CHUNK == 0
    nc = seq // rows
    tril = np.tril(np.ones((SSD_CHUNK, SSD_CHUNK), np.float32))
    hmask = np.zeros((2, 1, LANES), np.float32)
    hmask[0, 0, :SSD_HEADDIM] = 1.0
    hmask[1, 0, SSD_HEADDIM:] = 1.0
    expand = np.zeros((LANES, SSD_INNER), np.float32)
    for hd in range(SSD_HEADS):
        expand[hd, hd * SSD_HEADDIM:(hd + 1) * SSD_HEADDIM] = 1.0
    params = (wdr, cw, cb, dbc, dbr, alc, alr, ale, de, gn)
    consts = (jnp.asarray(tril, BF16), jnp.asarray(tril.T, BF16), jnp.asarray(hmask),
              jnp.asarray(expand, BF16))
    wspec = lambda name, width: _layer_spec(w_in, layer, 2, width, W_IN_COL[name])
    return pl.pallas_call(
        _ssd_kernel,
        grid=(bsz, nc),
        in_specs=[pl.BlockSpec((rows, d), lambda b, c: (b * nc + c, 0)),
                  wspec("z", SSD_INNER), wspec("xbc", SSD_CONV_DIM), wspec("dt", LANES)]
        + [_layer_spec(a, layer, 2) for a in params] + [_const_spec(a, 2) for a in consts],
        out_specs=pl.BlockSpec((rows, SSD_INNER), lambda b, c: (b * nc + c, 0)),
        out_shape=jax.ShapeDtypeStruct((n, SSD_INNER), BF16),
        scratch_shapes=[pltpu.VMEM((SSD_CONV_DIM // LANES, rows + 2 * SUBLANES, LANES), F32),
                        pltpu.VMEM((rows, SSD_CONV_DIM), F32),
                        pltpu.VMEM((rows, SSD_INNER), F32),
                        pltpu.VMEM((rows, SSD_INNER), F32),
                        pltpu.VMEM((rows, SSD_INNER), F32),
                        pltpu.VMEM((rows, LANES), F32),
                        pltpu.VMEM((SUBLANES, rows), F32),
                        pltpu.VMEM((SSD_STATE, SSD_INNER), F32)],
        compiler_params=_cparams(40, 2),
        name="ssd",
    )(u, w_in, w_in, w_in, *params, *consts)


def _merge_kernel(h_ref, xn_ref, wgate_ref, y0_ref, y1_ref, y2_ref, w0_ref, w1_ref, w2_ref,
                  wout_ref, o_ref):
    xn = xn_ref[...]
    d = h_ref.shape[1]
    merged = jnp.zeros(h_ref.shape, F32)
    for i, (y_ref, w_ref) in enumerate(((y0_ref, w0_ref), (y1_ref, w1_ref), (y2_ref, w2_ref))):
        gate = jax.nn.sigmoid(_dot(xn, wgate_ref[:, i * d:(i + 1) * d]))
        merged = merged + gate * _dot(y_ref[...], w_ref[...])
    o_ref[...] = h_ref[...] + _dot(merged.astype(BF16), wout_ref[...])


def _merge(h, xn, layer, w_in, ys, wbr, wout):
    n, d = h.shape
    rows = min(MERGE_ROWS, n)
    row_spec = pl.BlockSpec((rows, d), lambda i: (i, 0))
    y_specs = [pl.BlockSpec((rows, y.shape[1]), lambda i: (i, 0)) for y in ys]
    return pl.pallas_call(
        _merge_kernel,
        grid=(n // rows,),
        in_specs=[row_spec, row_spec, _layer_spec(w_in, layer, 1, 3 * d, W_IN_COL["gates"])] + y_specs
        + [_layer_spec(w, layer, 1) for w in wbr] + [_layer_spec(wout, layer, 1)],
        out_specs=row_spec,
        out_shape=jax.ShapeDtypeStruct((n, d), F32),
        compiler_params=_cparams(40, 1),
        name="merge",
    )(h, xn, w_in, *ys, *wbr, wout)


def _rows(v):
    return v.astype(F32)[:, None, :]


def _pad_last(w, width):
    return jnp.pad(w, [(0, 0)] * (w.ndim - 1) + [(0, width - w.shape[-1])])


def _pack_w_in(w_in):
    o = _IN_OFFS
    return jnp.concatenate([_pad_last(w_in[..., o[s0]:o[s1]], width)
                            for _, width, s0, s1 in _W_IN_PACK], axis=-1).astype(BF16)


def kernel(x, p, ffn1_norm, ffn1_gate, ffn1_up, ffn1_down, mix_norm, w_in, s5_lam_re, s5_lam_im, s5_log_step, s5_b_re, s5_b_im, s5_c_re, s5_c_im, s5_d, s5_glu, gla_gate_w2, gla_gate_b2, gla_norm, ssd_conv_w, ssd_conv_b, ssd_dt_bias, ssd_a_log, ssd_d, ssd_norm, w_br_s5, w_br_gla, w_br_ssd, w_out, ffn2_norm, ffn2_gate, ffn2_up, ffn2_down, ple_norm, ple_gate, ple_proj, final_norm):
    bsz, seq, d = x.shape
    depth = p.shape[0]
    n = bsz * seq
    bf = lambda w: w.astype(BF16)

    ffn1 = (_rows(ffn1_norm), bf(ffn1_gate), bf(ffn1_up), bf(ffn1_down))
    ffn2 = (_rows(ffn2_norm), bf(ffn2_gate), bf(ffn2_up), bf(ffn2_down))
    w_in_p = _pack_w_in(w_in)
    bblk, cblk, tab = jax.vmap(_s5_params)(s5_lam_re, s5_lam_im, s5_log_step, s5_b_re, s5_b_im,
                                           s5_c_re, s5_c_im)
    o = _IN_OFFS
    rep = SSD_HEADDIM
    gla_params = (bf(jnp.pad(gla_gate_w2, ((0, 0), (0, LANES - GLA_RANK), (0, 0)))),
                  _rows(gla_gate_b2), _rows(gla_norm))
    ssd_params = (bf(jnp.swapaxes(w_in[..., o[8]:o[9]], 1, 2)),
                  ssd_conv_w.astype(F32), _rows(ssd_conv_b),
                  _rows(_pad_last(ssd_dt_bias, LANES)), ssd_dt_bias.astype(F32)[:, :, None],
                  _rows(_pad_last(ssd_a_log, LANES)), ssd_a_log.astype(F32)[:, :, None],
                  _rows(jnp.repeat(ssd_a_log, rep, axis=-1)), _rows(jnp.repeat(ssd_d, rep, axis=-1)),
                  _rows(ssd_norm))
    branch_w = (bf(w_br_s5), bf(w_br_gla), bf(w_br_ssd))
    w_out_b = bf(w_out)
    mixn = _rows(mix_norm)
    s5_dskip = _rows(s5_d)
    s5_glu_b = bf(s5_glu)
    ple = (_rows(ple_norm), bf(ple_gate))
    p_rows = p.reshape(depth, n, p.shape[-1])
    ple_proj_b = bf(ple_proj)
    fn = final_norm.astype(F32).reshape(1, -1)

    h = x.reshape(n, d)
    for i in range(depth):
        h, u, y_s5 = _ffn_s5(h, seq, i, *ffn1, mixn, w_in_p, bblk, cblk, tab, s5_dskip, s5_glu_b)
        y_gla = _gla(u, bsz, seq, i, w_in_p, *gla_params)
        y_ssd = _ssd(u, bsz, seq, i, w_in_p, *ssd_params)
        h = _merge(h, u, i, w_in_p, (y_s5, y_gla, y_ssd), branch_w, w_out_b)
        h = _ffn_ple(h, i, *ffn2, *ple, p_rows, ple_proj_b, fn, i == depth - 1)
    return h.reshape(bsz, seq, d)
```
